```python
import jax, jax.numpy as jnp
from jax import lax
import numpy as np

D_MODEL = 2048
BATCH = 4
SEQ = 2048
DEPTH = 4

GRID_W = 64
CTX_LEN = 256
EPS = 1e-6
MLA_HEADS = D_MODEL // 256
QK_NOPE = 128
QK_ROPE = 64
V_DIM = 128
Q_LORA = D_MODEL // 4
KV_LORA = D_MODEL // 8
Q_BLOCK = 128
ROPE_BASE = 10000.0
AXIS_PAIRS = QK_ROPE // 4
ATTN_SCALE = (QK_NOPE + QK_ROPE) ** -0.5
HG_HEADS = D_MODEL // 256
HG_DK = 128
HG_DV = 128
HG_KW = HG_HEADS * HG_DK
HG_VW = HG_HEADS * HG_DV
CHUNK = 64
CONV_WIDTH = 31
D_FF = (11 * D_MODEL) // 4
FFN_CONV_WIDTH = 3
MOD_STD = 0.5
N_EVEN = (DEPTH + 1) // 2
N_ODD = DEPTH // 2
IN_SIZES = (Q_LORA, KV_LORA, QK_ROPE, HG_KW, HG_KW, HG_KW, HG_VW, HG_VW)
MIX_WIDTH = MLA_HEADS * V_DIM + HG_VW

kernel_name = "hybrid_mla_hgrn2_conformer_convffn_dit"


def rmsnorm(x, g):
    x32 = x.astype(jnp.float32)
    y = x32 * lax.rsqrt(jnp.mean(x32 * x32, axis=-1, keepdims=True) + EPS)
    return (y * g.astype(jnp.float32)).astype(x.dtype)


def layernorm(x, g, b):
    x32 = x.astype(jnp.float32)
    mu = jnp.mean(x32, axis=-1, keepdims=True)
    xc = x32 - mu
    var = jnp.mean(xc * xc, axis=-1, keepdims=True)
    return (xc * lax.rsqrt(var + EPS) * g.astype(jnp.float32) + b.astype(jnp.float32)).astype(x.dtype)


def dwconv(x, w):
    k = w.shape[0]
    pad = (k - 1) // 2
    return lax.conv_general_dilated(x, w[:, None, :].astype(x.dtype), (1,), [(pad, pad)],
                                    dimension_numbers=('NWC', 'WIO', 'NWC'),
                                    feature_group_count=x.shape[-1])


def split_cols(p, sizes):
    offs, acc = [], 0
    for s in sizes[:-1]:
        acc += s
        offs.append(acc)
    return jnp.split(p, offs, axis=-1)


def rope_tables(n):
    rows = n // GRID_W
    row = jnp.repeat(jnp.arange(rows), GRID_W)
    col = jnp.tile(jnp.arange(GRID_W), rows)
    pos = jnp.stack([row, col], axis=-1).astype(jnp.float32)
    freqs = ROPE_BASE ** (-jnp.arange(AXIS_PAIRS, dtype=jnp.float32) / AXIS_PAIRS)
    ang = pos[:, :, None] * freqs
    return jnp.cos(ang)[:, None], jnp.sin(ang)[:, None]


def apply_axial_rope(x, cos, sin):
    xs = x.astype(jnp.float32).reshape(*x.shape[:-1], 2, 2, AXIS_PAIRS)
    x1, x2 = xs[..., 0, :], xs[..., 1, :]
    out = jnp.stack([x1 * cos - x2 * sin, x2 * cos + x1 * sin], axis=-2)
    return out.reshape(x.shape)


def mla_qkv(cq, ckv, kr, q_norm, w_q_up, kv_norm, w_kv_up, rope):
    b, t, _ = cq.shape
    q = (rmsnorm(cq, q_norm) @ w_q_up).reshape(b, t, MLA_HEADS, QK_NOPE + QK_ROPE).astype(jnp.float32)
    kv = (rmsnorm(ckv, kv_norm) @ w_kv_up).reshape(b, t, MLA_HEADS, QK_NOPE + V_DIM).astype(jnp.float32)
    q_nope, q_rope = q[..., :QK_NOPE], q[..., QK_NOPE:]
    k_nope, v = kv[..., :QK_NOPE], kv[..., QK_NOPE:]
    k_rope = kr.astype(jnp.float32)[:, :, None, :]
    if rope is not None:
        cos, sin = rope
        q_rope = apply_axial_rope(q_rope, cos, sin)
        k_rope = apply_axial_rope(k_rope, cos, sin)
    k_rope = jnp.broadcast_to(k_rope, (b, t, MLA_HEADS, QK_ROPE))
    return (jnp.concatenate([q_nope, q_rope], axis=-1),
            jnp.concatenate([k_nope, k_rope], axis=-1), v)


def softmax_attend(q, k, v):
    s = jnp.einsum('bqhd,bkhd->bhqk', q, k) * ATTN_SCALE
    p = jax.nn.softmax(s, axis=-1)
    return jnp.einsum('bhqk,bkhd->bqhd', p, v)


def blocked_attend(q, k, v):
    b, t, h, d = q.shape
    nb = t // Q_BLOCK
    qb = q.reshape(b, nb, Q_BLOCK, h, d).swapaxes(0, 1)
    o = lax.map(lambda qq: softmax_attend(qq, k, v), qb)
    return o.swapaxes(0, 1).reshape(b, t, h, v.shape[-1])


def hgrn_gates(f_raw, lb):
    log_f = jnp.logaddexp(jnp.log(lb), jnp.log1p(-lb) + jax.nn.log_sigmoid(f_raw))
    k = (1.0 - lb) * jax.nn.sigmoid(-f_raw)
    return log_f, k


def gla_scan(q, k, v, log_f, s0):
    b, h, t, _ = q.shape
    dv = v.shape[-1]
    nc = t // CHUNK

    def chunks(a):
        return a.reshape(b, h, nc, CHUNK, a.shape[-1]).transpose(2, 0, 1, 3, 4)

    mask = jnp.tril(jnp.ones((CHUNK, CHUNK), dtype=bool))[:, :, None]

    def step(s, inp):
        qc, kc, vc, gc = inp
        cum = jnp.cumsum(gc, axis=2)
        inter = jnp.einsum('bhtk,bhkv->bhtv', qc * jnp.exp(cum), s)
        rel = cum[:, :, :, None, :] - cum[:, :, None, :, :]
        decay = jnp.where(mask, jnp.exp(jnp.minimum(rel, 0.0)), 0.0)
        att = jnp.einsum('bhtk,bhsk,bhtsk->bhts', qc, kc, decay)
        intra = jnp.einsum('bhts,bhsv->bhtv', att, vc)
        last = cum[:, :, -1:, :]
        s_new = (jnp.exp(last[:, :, 0, :, None]) * s
                 + jnp.einsum('bhsk,bhsv->bhkv', kc * jnp.exp(last - cum), vc))
        return s_new, inter + intra

    s_fin, o = lax.scan(step, s0, (chunks(q), chunks(k), chunks(v), chunks(log_f)))
    return s_fin, o.transpose(1, 2, 0, 3, 4).reshape(b, h, t, dv)


def bi_scan(q, k_f, lf_f, k_b, lf_b, v, s_f, s_b):
    flip = lambda a: jnp.flip(a, axis=2)
    sf, of = gla_scan(q, k_f, v, lf_f, s_f)
    sb, ob = gla_scan(flip(q), flip(k_b), flip(v), flip(lf_b), s_b)
    return sf, sb, of + flip(ob)


def mla_hgrn_mixer(u_lat, u_ctx, lb_f, lb_b, w_in, q_norm, w_q_up, kv_norm, w_kv_up, hg_norm, w_out, with_ctx):
    b, t, _ = u_lat.shape
    lat = split_cols(u_lat @ w_in, IN_SIZES)
    cx = split_cols(u_ctx @ w_in, IN_SIZES)
    q_l, k_l, v_l = mla_qkv(lat[0], lat[1], lat[2], q_norm, w_q_up, kv_norm, w_kv_up, rope_tables(t))
    q_c, k_c, v_c = mla_qkv(cx[0], cx[1], cx[2], q_norm, w_q_up, kv_norm, w_kv_up, None)
    att_l = blocked_attend(q_l, jnp.concatenate([k_c, k_l], axis=1), jnp.concatenate([v_c, v_l], axis=1))
    lbf = lb_f.reshape(HG_HEADS, 1, HG_DK)
    lbb = lb_b.reshape(HG_HEADS, 1, HG_DK)

    def heads(a, d):
        bb, tt, _ = a.shape
        return a.reshape(bb, tt, HG_HEADS, d).transpose(0, 2, 1, 3).astype(jnp.float32)

    def hg_inputs(parts):
        lf_f, k_f = hgrn_gates(heads(parts[4], HG_DK), lbf)
        lf_b, k_b = hgrn_gates(heads(parts[5], HG_DK), lbb)
        return heads(parts[3], HG_DK), k_f, lf_f, k_b, lf_b, heads(parts[6], HG_DV)

    zeros = jnp.zeros((b, HG_HEADS, HG_DK, HG_DV), jnp.float32)
    s_f, s_b, o_c = bi_scan(*hg_inputs(cx), zeros, zeros)
    _, _, o_l = bi_scan(*hg_inputs(lat), s_f, s_b)

    def merge(att, o, g):
        bb, tt, _ = g.shape
        hg = rmsnorm(o, hg_norm).transpose(0, 2, 1, 3).reshape(bb, tt, HG_VW).astype(g.dtype) * jax.nn.silu(g)
        return jnp.concatenate([att.reshape(bb, tt, -1).astype(g.dtype), hg], axis=-1) @ w_out

    y_lat = merge(att_l, o_l, lat[7])
    y_ctx = merge(softmax_attend(q_c, k_c, v_c), o_c, cx[7]) if with_ctx else None
    return y_lat, y_ctx


def conformer_conv(u, w_pw1, b_pw1, w_dw, b_dw, ln_g, ln_b, w_pw2, b_pw2):
    a, gate = jnp.split(u @ w_pw1 + b_pw1, 2, axis=-1)
    h = a * jax.nn.sigmoid(gate)
    h = dwconv(h, w_dw) + b_dw
    h = jax.nn.silu(layernorm(h, ln_g, ln_b))
    return h @ w_pw2 + b_pw2


def conv_ffn(u, w_up, w_conv, b_conv, w_down):
    h = dwconv(u @ w_up, w_conv) + b_conv
    val, gate = jnp.split(h, 2, axis=-1)
    return (jax.nn.silu(gate) * val) @ w_down


def setup_inputs(seed: int = 0) -> dict:
    key = jax.random.key(seed)
    ks = jax.random.split(key, 32)
    d = D_MODEL
    n_in = sum(IN_SIZES)

    def nrm(k, shape, s):
        return jax.random.normal(k, shape, jnp.float32) * s

    return {
        "x": nrm(ks[0], (BATCH, SEQ, d), 1.0),
        "c": nrm(ks[1], (BATCH, d), 1.0),
        "ctx": nrm(ks[2], (BATCH, CTX_LEN, d), 1.0),
        "c_ctx": nrm(ks[3], (d,), 1.0),
        "w_mod": nrm(ks[4], (DEPTH, d, 6 * d), MOD_STD * d ** -0.5),
        "b_mod": nrm(ks[5], (DEPTH, 6 * d), 0.01),
        "norm_gains": 1.0 + nrm(ks[6], (DEPTH, 4, d), 0.02),
        "w_in_ab": nrm(ks[7], (N_EVEN, d, n_in), d ** -0.5),
        "mla_q_norm": 1.0 + nrm(ks[8], (N_EVEN, Q_LORA), 0.02),
        "w_q_up": nrm(ks[9], (N_EVEN, Q_LORA, MLA_HEADS * (QK_NOPE + QK_ROPE)), Q_LORA ** -0.5),
        "mla_kv_norm": 1.0 + nrm(ks[10], (N_EVEN, KV_LORA), 0.02),
        "w_kv_up": nrm(ks[11], (N_EVEN, KV_LORA, MLA_HEADS * (QK_NOPE + V_DIM)), KV_LORA ** -0.5),
        "hgrn_lb": nrm(ks[12], (2, N_EVEN, HG_KW), 0.1),
        "hgrn_norm": 1.0 + nrm(ks[13], (N_EVEN, HG_DV), 0.02),
        "w_out_ab": nrm(ks[14], (N_EVEN, MIX_WIDTH, d), MIX_WIDTH ** -0.5),
        "conv_w_pw1": nrm(ks[15], (N_ODD, d, 2 * d), d ** -0.5),
        "conv_b_pw1": nrm(ks[16], (N_ODD, 2 * d), 0.01),
        "conv_w_dw": nrm(ks[17], (N_ODD, CONV_WIDTH, d), CONV_WIDTH ** -0.5),
        "conv_b_dw": nrm(ks[18], (N_ODD, d), 0.01),
        "conv_ln_g": 1.0 + nrm(ks[19], (N_ODD, d), 0.02),
        "conv_ln_b": nrm(ks[20], (N_ODD, d), 0.01),
        "conv_w_pw2": nrm(ks[21], (N_ODD, d, d), d ** -0.5),
        "conv_b_pw2": nrm(ks[22], (N_ODD, d), 0.01),
        "ffn_w_up": nrm(ks[23], (DEPTH, d, 2 * D_FF), d ** -0.5),
        "ffn_w_conv": nrm(ks[24], (DEPTH, FFN_CONV_WIDTH, 2 * D_FF), FFN_CONV_WIDTH ** -0.5),
        "ffn_b_conv": nrm(ks[25], (DEPTH, 2 * D_FF), 0.01),
        "ffn_w_down": nrm(ks[26], (DEPTH, D_FF, d), D_FF ** -0.5),
    }


def reference(x, c, ctx, c_ctx, w_mod, b_mod, norm_gains, w_in_ab, mla_q_norm, w_q_up, mla_kv_norm, w_kv_up,
              hgrn_lb, hgrn_norm, w_out_ab, conv_w_pw1, conv_b_pw1, conv_w_dw, conv_b_dw, conv_ln_g, conv_ln_b,
              conv_w_pw2, conv_b_pw2, ffn_w_up, ffn_w_conv, ffn_b_conv, ffn_w_down):
    lb = jnp.cumsum(jax.nn.softmax(hgrn_lb.astype(jnp.float32), axis=1), axis=1)
    lb = lb - lb[:, :1]
    sc = jax.nn.silu(c)
    scc = jax.nn.silu(c_ctx)
    h_ctx = ctx
    for l in range(DEPTH):
        last = l == DEPTH - 1
        even = l % 2 == 0
        j = l // 2
        m = [mm[:, None, :] for mm in jnp.split(sc @ w_mod[l] + b_mod[l], 6, axis=-1)]
        mc = jnp.split(scc @ w_mod[l] + b_mod[l], 6, axis=-1)
        g_pre1, g_post1, g_pre2, g_post2 = norm_gains[l]
        ffn_p = (ffn_w_up[l], ffn_w_conv[l], ffn_b_conv[l], ffn_w_down[l])
        u_lat = rmsnorm(x, g_pre1) * (1.0 + m[1]) + m[0]
        u_ctx = rmsnorm(h_ctx, g_pre1) * (1.0 + mc[1]) + mc[0] if (even or not last) else None
        if even:
            y_lat, y_ctx = mla_hgrn_mixer(u_lat, u_ctx, lb[0, j], lb[1, j], w_in_ab[j], mla_q_norm[j], w_q_up[j],
                                          mla_kv_norm[j], w_kv_up[j], hgrn_norm[j], w_out_ab[j], not last)
        else:
            conv_p = (conv_w_pw1[j], conv_b_pw1[j], conv_w_dw[j], conv_b_dw[j], conv_ln_g[j], conv_ln_b[j],
                      conv_w_pw2[j], conv_b_pw2[j])
            y_lat = conformer_conv(u_lat, *conv_p)
            y_ctx = None if last else conformer_conv(u_ctx, *conv_p)
        x = x + m[2] * rmsnorm(y_lat, g_post1)
        u_lat = rmsnorm(x, g_pre2) * (1.0 + m[4]) + m[3]
        x = x + m[5] * rmsnorm(conv_ffn(u_lat, *ffn_p), g_post2)
        if not last:
            h_ctx = h_ctx + mc[2] * rmsnorm(y_ctx, g_post1)
            u_ctx = rmsnorm(h_ctx, g_pre2) * (1.0 + mc[4]) + mc[3]
            h_ctx = h_ctx + mc[5] * rmsnorm(conv_ffn(u_ctx, *ffn_p), g_post2)
    return x
```

```python
import functools

import numpy as np
import jax
import jax.numpy as jnp
from jax import lax
from jax.experimental import pallas as pl
from jax.experimental.pallas import tpu as pltpu

F32 = jnp.float32
BF16 = jnp.bfloat16

D = 2048
H = D // 256
DH = 128
ROPE = 64
QL = D // 4
KVL = D // 8
DFF = (11 * D) // 4
GRID_W = 64
ROPE_BASE = 10000.0
CONV_W = 31
CONV_HALO = 16
EPS = 1e-6
ATTN_SCALE = (DH + ROPE) ** -0.5
P_FRONT = 1024
P_WIDTH = P_FRONT + 5 * H * DH
GLA_C = 128
GLA_LEVELS = 7
VMEM_LIMIT = 56 * 1024 * 1024


def _cp(*sem):
    return pltpu.CompilerParams(dimension_semantics=sem, vmem_limit_bytes=VMEM_LIMIT)


def _sigmoid(x):
    return 1.0 / (1.0 + jnp.exp(-x))


def _rms(x):
    return x * lax.rsqrt(jnp.mean(x * x, axis=-1, keepdims=True) + EPS)


def _dot(a, b):
    return jnp.dot(a, b, preferred_element_type=F32)


def _dot_nt(a, b):
    return lax.dot_general(a, b, (((1,), (1,)), ((), ())), preferred_element_type=F32)


def _dot_tn(a, b):
    return lax.dot_general(a, b, (((0,), (0,)), ((), ())), preferred_element_type=F32)


def _mod_row(i, tm, n_lat, seq):
    return jnp.where(i < n_lat // tm, i // (seq // tm), n_lat // seq)


def _mod_body(c_ref, w_ref, b_ref, o_ref):
    c = c_ref[...]
    s = (c * _sigmoid(c)).astype(BF16)
    o_ref[0] = _dot(s, w_ref[0].astype(BF16)) + b_ref[0]


def modulation(cc, w_mod, b_mod):
    depth, _, n = w_mod.shape
    rows = cc.shape[0]
    tn = 1024
    return pl.pallas_call(
        _mod_body,
        grid=(depth, n // tn),
        in_specs=[pl.BlockSpec((rows, D), lambda l, j: (0, 0)),
                  pl.BlockSpec((1, D, tn), lambda l, j: (l, 0, j)),
                  pl.BlockSpec((1, 1, tn), lambda l, j: (l, 0, j))],
        out_specs=pl.BlockSpec((1, rows, tn), lambda l, j: (l, 0, j)),
        out_shape=jax.ShapeDtypeStruct((depth, rows, n), F32),
        compiler_params=_cp("parallel", "parallel"),
    )(cc, w_mod, b_mod.reshape(depth, 1, n))


def _norm_mod_body(x_ref, g_ref, m_ref, o_ref, *, shift, scale):
    m = m_ref[0]
    y = _rms(x_ref[...]) * g_ref[...]
    o_ref[...] = (y * (1.0 + m[:, scale * D:(scale + 1) * D]) + m[:, shift * D:(shift + 1) * D]).astype(BF16)


def norm_mod(x, gain, mods, n_lat, seq, shift, scale):
    m = x.shape[0]
    tm = 256
    return pl.pallas_call(
        functools.partial(_norm_mod_body, shift=shift, scale=scale),
        grid=(m // tm,),
        in_specs=[pl.BlockSpec((tm, D), lambda i: (i, 0)),
                  pl.BlockSpec((1, D), lambda i: (0, 0)),
                  pl.BlockSpec((1, 1, 6 * D), lambda i: (_mod_row(i, tm, n_lat, seq), 0, 0))],
        out_specs=pl.BlockSpec((tm, D), lambda i: (i, 0)),
        out_shape=jax.ShapeDtypeStruct((m, D), BF16),
        compiler_params=_cp("parallel"),
    )(x, gain.reshape(1, D), mods)


def _mm_body(x_ref, w_ref, o_ref):
    o_ref[...] = _dot(x_ref[...], w_ref[...])


def matmul(x, w, tm, tn):
    m, k = x.shape
    n = w.shape[1]
    return pl.pallas_call(
        _mm_body,
        grid=(m // tm, n // tn),
        in_specs=[pl.BlockSpec((tm, k), lambda i, j: (i, 0)),
                  pl.BlockSpec((k, tn), lambda i, j: (0, j))],
        out_specs=pl.BlockSpec((tm, tn), lambda i, j: (i, j)),
        out_shape=jax.ShapeDtypeStruct((m, n), F32),
        compiler_params=_cp("parallel", "parallel"),
    )(x, w)


def _mla_prep_body(p_ref, qn_ref, wq_ref, kvn_ref, wkv_ref, cos_ref, sin_ref, q_ref, k_ref, v_ref):
    p = p_ref[...]
    tm = p.shape[0]
    cos = cos_ref[...]
    sin = sin_ref[...]
    lane = lax.broadcasted_iota(jnp.int32, (tm, DH), 1)
    first_half = (lane & 31) < 16

    def rope(x):
        partner = jnp.where(first_half, pltpu.roll(x, DH - 16, 1), pltpu.roll(x, 16, 1))
        return x * cos + partner * sin

    cq = (_rms(p[:, :QL]) * qn_ref[...]).astype(BF16)
    ckv = (_rms(p[:, QL:QL + KVL]) * kvn_ref[...]).astype(BF16)
    kr = rope(p[:, QL + KVL:QL + KVL + DH]).astype(BF16)
    q = _dot(cq, wq_ref[...])
    kv = _dot(ckv, wkv_ref[...])
    for h in range(H):
        lo = 2 * DH * h
        q_ref[:, lo:lo + DH] = (q[:, lo:lo + DH] * ATTN_SCALE).astype(BF16)
        q_ref[:, lo + DH:lo + 2 * DH] = (rope(q[:, lo + DH:lo + 2 * DH]) * ATTN_SCALE).astype(BF16)
        k_ref[:, lo:lo + DH] = kv[:, DH * h:DH * (h + 1)].astype(BF16)
        k_ref[:, lo + DH:lo + 2 * DH] = kr
    v_ref[...] = kv[:, H * DH:].astype(BF16)


def mla_prep(p, q_norm, wq, kv_norm, wkv, cos, sin):
    m = p.shape[0]
    tm = 512
    row = lambda i: (i, 0)
    fixed = lambda i: (0, 0)
    return pl.pallas_call(
        _mla_prep_body,
        grid=(m // tm,),
        in_specs=[pl.BlockSpec((tm, P_FRONT), row),
                  pl.BlockSpec((1, QL), fixed),
                  pl.BlockSpec((QL, 2 * H * DH), fixed),
                  pl.BlockSpec((1, KVL), fixed),
                  pl.BlockSpec((KVL, 2 * H * DH), fixed),
                  pl.BlockSpec((tm, DH), row),
                  pl.BlockSpec((tm, DH), row)],
        out_specs=[pl.BlockSpec((tm, 2 * H * DH), row),
                   pl.BlockSpec((tm, 2 * H * DH), row),
                   pl.BlockSpec((tm, H * DH), row)],
        out_shape=[jax.ShapeDtypeStruct((m, 2 * H * DH), BF16),
                   jax.ShapeDtypeStruct((m, 2 * H * DH), BF16),
                   jax.ShapeDtypeStruct((m, H * DH), BF16)],
        compiler_params=_cp("parallel"),
    )(p, q_norm.reshape(1, QL), wq, kv_norm.reshape(1, KVL), wkv, cos, sin)


def _attn_lat_body(q_ref, kl_ref, vl_ref, kc_ref, vc_ref, o_ref):
    q = q_ref[...]
    sl = _dot_nt(q, kl_ref[...])
    sc = _dot_nt(q, kc_ref[...])
    mx = jnp.maximum(jnp.max(sl, axis=-1, keepdims=True), jnp.max(sc, axis=-1, keepdims=True))
    el = jnp.exp(sl - mx)
    ec = jnp.exp(sc - mx)
    den = jnp.sum(el, axis=-1, keepdims=True) + jnp.sum(ec, axis=-1, keepdims=True)
    o = _dot(el.astype(BF16), vl_ref[...]) + _dot(ec.astype(BF16), vc_ref[...])
    o_ref[...] = (o / den).astype(BF16)


def _attn_ctx_body(q_ref, kc_ref, vc_ref, o_ref):
    sc = _dot_nt(q_ref[...], kc_ref[...])
    ec = jnp.exp(sc - jnp.max(sc, axis=-1, keepdims=True))
    o = _dot(ec.astype(BF16), vc_ref[...])
    o_ref[...] = (o / jnp.sum(ec, axis=-1, keepdims=True)).astype(BF16)


def attention_latent(q, k, v, b, t, tc):
    n_lat = b * t
    tq = 512
    nq = t // tq
    cb = n_lat // tc
    return pl.pallas_call(
        _attn_lat_body,
        grid=(b, H, nq),
        in_specs=[pl.BlockSpec((tq, 2 * DH), lambda bi, h, qi: (bi * nq + qi, h)),
                  pl.BlockSpec((t, 2 * DH), lambda bi, h, qi: (bi, h)),
                  pl.BlockSpec((t, DH), lambda bi, h, qi: (bi, h)),
                  pl.BlockSpec((tc, 2 * DH), lambda bi, h, qi: (cb + bi, h)),
                  pl.BlockSpec((tc, DH), lambda bi, h, qi: (cb + bi, h))],
        out_specs=pl.BlockSpec((tq, DH), lambda bi, h, qi: (bi * nq + qi, h)),
        out_shape=jax.ShapeDtypeStruct((n_lat, H * DH), BF16),
        compiler_params=_cp("parallel", "parallel", "parallel"),
    )(q, k, v, k, v)


def attention_context(q, k, v, b, t, tc):
    cb = (b * t) // tc
    return pl.pallas_call(
        _attn_ctx_body,
        grid=(b, H),
        in_specs=[pl.BlockSpec((tc, 2 * DH), lambda bi, h: (cb + bi, h)),
                  pl.BlockSpec((tc, 2 * DH), lambda bi, h: (cb + bi, h)),
                  pl.BlockSpec((tc, DH), lambda bi, h: (cb + bi, h))],
        out_specs=pl.BlockSpec((tc, DH), lambda bi, h: (bi, h)),
        out_shape=jax.ShapeDtypeStruct((b * tc, H * DH), BF16),
        compiler_params=_cp("parallel", "parallel"),
    )(q, k, v)


def _gla_tables():
    c = GLA_C
    t = np.arange(c)[:, None]
    r = np.arange(c)[None, :]
    mats = []
    for lv in range(GLA_LEVELS):
        same = (t >> (lv + 1)) == (r >> (lv + 1))
        bt = (t >> lv) & 1
        br = (r >> lv) & 1
        upper = same & (bt == 1) & (br == 1) & (r <= t)
        lower = same & (bt == 0) & (br == 0) & (r > t)
        mats.append(upper | lower)
    mats.append(r <= t)
    mats.append(r > t)
    fwd = np.stack(mats).astype(np.float32)
    bwd = fwd[:, ::-1, ::-1]
    masks = np.stack([fwd, bwd]).reshape(2, (GLA_LEVELS + 2) * c, c)
    x = t ^ r
    lvl = np.where(x == 0, -1, np.floor(np.log2(np.maximum(x, 1)))).astype(np.int32)
    return masks, lvl


def _gla_body(q_ref, f_ref, v_ref, lb_ref, msk_ref, lvl_ref, o_ref, st_ref, *, layer):
    c = GLA_C
    d = pl.program_id(2)

    @pl.when(pl.program_id(3) == 0)
    def _():
        st_ref[...] = jnp.zeros_like(st_ref)

    q = q_ref[...]
    x = f_ref[...]
    v = v_ref[...].astype(BF16)
    log_sig = jnp.minimum(x, 0.0) - jnp.log1p(jnp.exp(-jnp.abs(x)))
    if layer == 0:
        g = log_sig
        kk = 1.0 / (1.0 + jnp.exp(x))
    else:
        raw = lb_ref[0]
        e = jnp.exp(raw - jnp.max(raw, axis=0, keepdims=True))
        prob = e / jnp.sum(e, axis=0, keepdims=True)
        lb = jnp.sum(prob[1:layer + 1], axis=0, keepdims=True)
        a = jnp.log(lb)
        bb = jnp.log1p(-lb) + log_sig
        g = jnp.maximum(a, bb) + jnp.log1p(jnp.exp(-jnp.abs(a - bb)))
        kk = (1.0 - lb) / (1.0 + jnp.exp(x))

    g_hi = g.astype(BF16)
    g_lo = (g - g_hi.astype(F32)).astype(BF16)
    e2 = _dot(msk_ref[0], jnp.concatenate([g_hi, g_lo], axis=1))
    ex = jnp.exp(e2[:, :DH] + e2[:, DH:])

    lvl = lvl_ref[...]
    row = lax.broadcasted_iota(jnp.int32, (c, DH), 0)
    att = jnp.where(lvl == -1, _dot_nt(q.astype(BF16), kk.astype(BF16)), 0.0)
    for lv in range(GLA_LEVELS):
        is_q = ((row >> lv) & 1) == (1 - d)
        scaled = jnp.where(is_q, q, kk) * ex[lv * c:(lv + 1) * c]
        ql = jnp.where(is_q, scaled, 0.0).astype(BF16)
        kl = jnp.where(is_q, 0.0, scaled).astype(BF16)
        att = jnp.where(lvl == lv, _dot_nt(ql, kl), att)

    q_in = (q * ex[GLA_LEVELS * c:(GLA_LEVELS + 1) * c]).astype(BF16)
    k_out = (kk * ex[(GLA_LEVELS + 1) * c:]).astype(BF16)
    st = st_ref[...]
    o_ref[0] = _dot(att.astype(BF16), v) + _dot_nt(q_in, st.astype(BF16))
    st_ref[...] = st * jnp.exp(jnp.sum(g, axis=0, keepdims=True)) + _dot_tn(v, k_out)


def gla(p, lb_raw, masks, lvl, b, t, tc, layer):
    c = GLA_C
    m = p.shape[0]
    n_ctx_chunks = tc // c
    n_lat_chunks = t // c
    n_chunks = n_ctx_chunks + n_lat_chunks
    ctx0 = (b * t) // c
    n_even = lb_raw.shape[1]

    def blk(bi, d, i):
        jc = jnp.where(d == 0, i, n_ctx_chunks - 1 - i)
        jl = jnp.where(d == 0, i - n_ctx_chunks, n_chunks - 1 - i)
        return jnp.where(i < n_ctx_chunks, ctx0 + bi * n_ctx_chunks + jc, bi * n_lat_chunks + jl)

    q_col = P_FRONT // DH
    return pl.pallas_call(
        functools.partial(_gla_body, layer=layer),
        grid=(b, H, 2, n_chunks),
        in_specs=[pl.BlockSpec((c, DH), lambda bi, h, d, i: (blk(bi, d, i), q_col + h)),
                  pl.BlockSpec((c, DH), lambda bi, h, d, i: (blk(bi, d, i), q_col + H * (1 + d) + h)),
                  pl.BlockSpec((c, DH), lambda bi, h, d, i: (blk(bi, d, i), q_col + 3 * H + h)),
                  pl.BlockSpec((1, n_even, DH), lambda bi, h, d, i: (d, 0, h)),
                  pl.BlockSpec((1, (GLA_LEVELS + 2) * c, c), lambda bi, h, d, i: (d, 0, 0)),
                  pl.BlockSpec((c, c), lambda bi, h, d, i: (0, 0))],
        out_specs=pl.BlockSpec((1, c, DH), lambda bi, h, d, i: (d, blk(bi, d, i), h)),
        out_shape=jax.ShapeDtypeStruct((2, m, H * DH), F32),
        scratch_shapes=[pltpu.VMEM((DH, DH), F32)],
        compiler_params=_cp("parallel", "parallel", "parallel", "arbitrary"),
    )(p, p, p, lb_raw, masks, lvl)


def _hg_merge_body(o_ref, g_ref, n_ref, y_ref):
    o = o_ref[0] + o_ref[1]
    gate = g_ref[...]
    gain = n_ref[...]
    for h in range(H):
        sl = slice(h * DH, (h + 1) * DH)
        gh = gate[:, sl]
        y_ref[:, sl] = (_rms(o[:, sl]) * gain * (gh * _sigmoid(gh))).astype(BF16)


def hg_merge(o, p, hg_norm):
    m = p.shape[0]
    tm = 512
    w = H * DH
    return pl.pallas_call(
        _hg_merge_body,
        grid=(m // tm,),
        in_specs=[pl.BlockSpec((2, tm, w), lambda i: (0, i, 0)),
                  pl.BlockSpec((tm, w), lambda i: (i, (P_WIDTH - w) // w)),
                  pl.BlockSpec((1, DH), lambda i: (0, 0))],
        out_specs=pl.BlockSpec((tm, w), lambda i: (i, 0)),
        out_shape=jax.ShapeDtypeStruct((m, w), BF16),
        compiler_params=_cp("parallel"),
    )(o, p, hg_norm.reshape(1, DH))


def _mm_res_body(*refs, gate, shift, scale, nk, has_next):
    if has_next:
        a_ref, w_ref, b_ref, x_ref, gp_ref, mc_ref, gn_ref, mn_ref, xo_ref, uo_ref, acc_ref = refs
    else:
        a_ref, w_ref, b_ref, x_ref, gp_ref, mc_ref, xo_ref, acc_ref = refs
    k = pl.program_id(1)

    @pl.when(k == 0)
    def _():
        acc_ref[...] = jnp.zeros_like(acc_ref)

    acc_ref[...] += _dot(a_ref[...], w_ref[...])

    @pl.when(k == nk - 1)
    def _():
        mc = mc_ref[0]
        y = _rms(acc_ref[...] + b_ref[...]) * gp_ref[...]
        xn = x_ref[...] + mc[:, gate * D:(gate + 1) * D] * y
        xo_ref[...] = xn
        if has_next:
            mn = mn_ref[0]
            u = _rms(xn) * gn_ref[...]
            uo_ref[...] = (u * (1.0 + mn[:, scale * D:(scale + 1) * D]) + mn[:, shift * D:(shift + 1) * D]).astype(BF16)


def matmul_residual(a, w, bias, x, g_post, mods_cur, gate, n_lat, seq, nxt=None):
    m, kdim = a.shape
    tm = 512
    tk = 512
    nk = kdim // tk
    mrow = lambda i, k: (_mod_row(i, tm, n_lat, seq), 0, 0)
    row = lambda i, k: (i, 0)
    fixed = lambda i, k: (0, 0)
    in_specs = [pl.BlockSpec((tm, tk), lambda i, k: (i, k)),
                pl.BlockSpec((tk, D), lambda i, k: (k, 0)),
                pl.BlockSpec((1, D), fixed),
                pl.BlockSpec((tm, D), row),
                pl.BlockSpec((1, D), fixed),
                pl.BlockSpec((1, 1, 6 * D), mrow)]
    args = [a, w, bias.reshape(1, D), x, g_post.reshape(1, D), mods_cur]
    out_specs = [pl.BlockSpec((tm, D), row)]
    out_shape = [jax.ShapeDtypeStruct((m, D), F32)]
    shift = scale = 0
    if nxt is not None:
        g_next, mods_next, shift, scale = nxt
        in_specs += [pl.BlockSpec((1, D), fixed), pl.BlockSpec((1, 1, 6 * D), mrow)]
        args += [g_next.reshape(1, D), mods_next]
        out_specs.append(pl.BlockSpec((tm, D), row))
        out_shape.append(jax.ShapeDtypeStruct((m, D), BF16))
    outs = pl.pallas_call(
        functools.partial(_mm_res_body, gate=gate, shift=shift, scale=scale, nk=nk, has_next=nxt is not None),
        grid=(m // tm, nk),
        in_specs=in_specs,
        out_specs=out_specs,
        out_shape=out_shape,
        scratch_shapes=[pltpu.VMEM((tm, D), F32)],
        compiler_params=_cp("parallel", "arbitrary"),
    )(*args)
    return (outs[0], outs[1]) if nxt is not None else (outs[0], None)


def _mm_gated_body(x_ref, wa_ref, wg_ref, ca_ref, cg_ref, ba_ref, bg_ref, o_ref, *, seq, conv, silu):
    x = x_ref[...]
    za = _dot(x, wa_ref[...])
    zg = _dot(x, wg_ref[...])
    if conv:
        tm = za.shape[0]
        pos = lax.broadcasted_iota(jnp.int32, za.shape, 0) & (seq - 1)
        at_start = pos == 0
        at_end = pos == seq - 1

        def conv3(z, cw):
            prev = jnp.where(at_start, 0.0, pltpu.roll(z, 1, 0))
            nxt = jnp.where(at_end, 0.0, pltpu.roll(z, tm - 1, 0))
            return cw[0:1] * prev + cw[1:2] * z + cw[2:3] * nxt

        za = conv3(za, ca_ref[...])
        zg = conv3(zg, cg_ref[...])
    za = za + ba_ref[...]
    zg = zg + bg_ref[...]
    sg = _sigmoid(zg)
    o_ref[...] = (za * (zg * sg if silu else sg)).astype(o_ref.dtype)


def matmul_gated(x, w, conv_w, bias, *, row0, m, tm, tn, seq, conv, silu, out_dtype):
    kdim = x.shape[1]
    n = w.shape[1] // 2
    nj = n // tn
    i0 = row0 // tm
    first = lambda i, j: (0, j)
    second = lambda i, j: (0, j + nj)
    bias = bias.reshape(1, 2 * n)
    return pl.pallas_call(
        functools.partial(_mm_gated_body, seq=seq, conv=conv, silu=silu),
        grid=(m // tm, nj),
        in_specs=[pl.BlockSpec((tm, kdim), lambda i, j: (i + i0, 0)),
                  pl.BlockSpec((kdim, tn), first),
                  pl.BlockSpec((kdim, tn), second),
                  pl.BlockSpec((conv_w.shape[0], tn), first),
                  pl.BlockSpec((conv_w.shape[0], tn), second),
                  pl.BlockSpec((1, tn), first),
                  pl.BlockSpec((1, tn), second)],
        out_specs=pl.BlockSpec((tm, tn), lambda i, j: (i, j)),
        out_shape=jax.ShapeDtypeStruct((m, n), out_dtype),
        compiler_params=_cp("parallel", "parallel"),
    )(x, w, w, conv_w, conv_w, bias, bias)


def _dwconv_ln_body(prev_ref, cur_ref, next_ref, w_ref, b_ref, g_ref, beta_ref, o_ref, ext_ref, acc_ref,
                    *, tiles_per_seq, n_lat_tiles):
    i = pl.program_id(0)
    tm = cur_ref.shape[0]
    hw = CONV_HALO
    is_lat = i < n_lat_tiles
    j = i % tiles_per_seq
    has_prev = jnp.logical_and(is_lat, j > 0).astype(F32)
    has_next = jnp.logical_and(is_lat, j < tiles_per_seq - 1).astype(F32)
    ext_ref[0:hw, :] = prev_ref[tm - hw:, :] * has_prev
    ext_ref[hw:hw + tm, :] = cur_ref[...]
    ext_ref[hw + tm:, :] = next_ref[0:hw, :] * has_next

    def lane_block(cb, carry):
        col = pl.multiple_of(cb * DH, DH)
        wv = w_ref[:, pl.ds(col, DH)]
        for r0 in range(0, tm, 128):
            acc = jnp.zeros((128, DH), F32)
            for k in range(CONV_W):
                off = r0 + k + hw - (CONV_W - 1) // 2
                acc = acc + ext_ref[off:off + 128, pl.ds(col, DH)] * wv[k:k + 1]
            acc_ref[r0:r0 + 128, pl.ds(col, DH)] = acc
        return carry

    lax.fori_loop(0, D // DH, lane_block, 0)
    hcv = acc_ref[...] + b_ref[...]
    mu = jnp.mean(hcv, axis=-1, keepdims=True)
    xc = hcv - mu
    var = jnp.mean(xc * xc, axis=-1, keepdims=True)
    y = xc * lax.rsqrt(var + EPS) * g_ref[...] + beta_ref[...]
    o_ref[...] = (y * _sigmoid(y)).astype(BF16)


def dwconv_ln(hx, w_dw, b_dw, ln_g, ln_b, n_lat, seq):
    m = hx.shape[0]
    tm = 256
    nt = m // tm
    fixed = lambda i: (0, 0)
    vec = lambda a: a.reshape(1, D)
    return pl.pallas_call(
        functools.partial(_dwconv_ln_body, tiles_per_seq=seq // tm, n_lat_tiles=n_lat // tm),
        grid=(nt,),
        in_specs=[pl.BlockSpec((tm, D), lambda i: (jnp.maximum(i - 1, 0), 0)),
                  pl.BlockSpec((tm, D), lambda i: (i, 0)),
                  pl.BlockSpec((tm, D), lambda i: (jnp.minimum(i + 1, nt - 1), 0)),
                  pl.BlockSpec((CONV_W, D), fixed),
                  pl.BlockSpec((1, D), fixed),
                  pl.BlockSpec((1, D), fixed),
                  pl.BlockSpec((1, D), fixed)],
        out_specs=pl.BlockSpec((tm, D), lambda i: (i, 0)),
        out_shape=jax.ShapeDtypeStruct((m, D), BF16),
        scratch_shapes=[pltpu.VMEM((tm + 2 * CONV_HALO, D), F32), pltpu.VMEM((tm, D), F32)],
        compiler_params=_cp("parallel"),
    )(hx, hx, hx, w_dw, vec(b_dw), vec(ln_g), vec(ln_b))


def _rope_tables(b, t, tc):
    rows = t // GRID_W
    row = jnp.repeat(jnp.arange(rows), GRID_W).astype(F32)
    col = jnp.tile(jnp.arange(GRID_W), rows).astype(F32)
    pairs = ROPE // 4
    freqs = ROPE_BASE ** (-jnp.arange(pairs, dtype=F32) / pairs)
    ar = row[:, None] * freqs
    ac = col[:, None] * freqs
    cos = jnp.concatenate([jnp.cos(ar), jnp.cos(ar), jnp.cos(ac), jnp.cos(ac), jnp.ones((t, DH - ROPE), F32)], axis=1)
    sin = jnp.concatenate([-jnp.sin(ar), jnp.sin(ar), -jnp.sin(ac), jnp.sin(ac), jnp.zeros((t, DH - ROPE), F32)], axis=1)
    cos = jnp.concatenate([jnp.tile(cos, (b, 1)), jnp.ones((b * tc, DH), F32)], axis=0)
    sin = jnp.concatenate([jnp.tile(sin, (b, 1)), jnp.zeros((b * tc, DH), F32)], axis=0)
    return cos, sin


def _pick(n, *cands):
    for cand in cands:
        if n % cand == 0:
            return cand
    raise ValueError(f"no tile for {n}")


def kernel(x, c, ctx, c_ctx, w_mod, b_mod, norm_gains, w_in_ab, mla_q_norm, w_q_up, mla_kv_norm, w_kv_up, hgrn_lb, hgrn_norm, w_out_ab, conv_w_pw1, conv_b_pw1, conv_w_dw, conv_b_dw, conv_ln_g, conv_ln_b, conv_w_pw2, conv_b_pw2, ffn_w_up, ffn_w_conv, ffn_b_conv, ffn_w_down):
    b, t, _ = x.shape
    tc = ctx.shape[1]
    depth = w_mod.shape[0]
    n_lat = b * t
    n_ctx = b * tc
    assert x.shape[2] == D and t % 512 == 0 and tc % 256 == 0 and n_ctx % 512 == 0 and tc & (tc - 1) == 0

    hcur = jnp.concatenate([x.reshape(n_lat, D), ctx.reshape(n_ctx, D)], axis=0)
    cc = jnp.concatenate([c, c_ctx[None, :], jnp.zeros((7 - b, D), F32)], axis=0)
    mods = modulation(cc, w_mod, b_mod).reshape(depth, 8, 1, 6 * D)
    cos, sin = _rope_tables(b, t, tc)
    masks_np, lvl_np = _gla_tables()
    masks = jnp.asarray(masks_np, BF16)
    lvl = jnp.asarray(lvl_np)
    zero_bias = jnp.zeros((D,), F32)

    u = norm_mod(hcur, norm_gains[0, 0], mods[0], n_lat, t, shift=0, scale=1)
    for l in range(depth):
        last = l == depth - 1
        j = l // 2
        if last:
            hcur, u = hcur[:n_lat], u[:n_lat]
        m = hcur.shape[0]
        g_pre1, g_post1, g_pre2, g_post2 = norm_gains[l]
        if l % 2 == 0:
            w_in = w_in_ab[j]
            w_in_p = jnp.concatenate([w_in[:, :QL + KVL + ROPE], jnp.zeros((D, P_FRONT - QL - KVL - ROPE), F32),
                                      w_in[:, QL + KVL + ROPE:]], axis=1).astype(BF16)
            wq = jnp.pad(w_q_up[j].reshape(QL, H, DH + ROPE), ((0, 0), (0, 0), (0, DH - ROPE)))
            wq = wq.reshape(QL, 2 * H * DH).astype(BF16)
            wkv = w_kv_up[j].reshape(KVL, H, 2 * DH)
            wkv = jnp.concatenate([wkv[:, :, :DH].reshape(KVL, H * DH), wkv[:, :, DH:].reshape(KVL, H * DH)], axis=1)
            p = matmul(u, w_in_p, _pick(m, 1024, 512), 512)
            q, k, v = mla_prep(p, mla_q_norm[j], wq, mla_kv_norm[j], wkv.astype(BF16), cos, sin)
            att = jnp.concatenate([attention_latent(q, k, v, b, t, tc), attention_context(q, k, v, b, t, tc)], axis=0)
            o = gla(p, hgrn_lb[:, :, :], masks, lvl, b, t, tc, j)
            hg = hg_merge(o, p, hgrn_norm[j])
            mix = jnp.concatenate([att, hg], axis=1)
            w_o, b_o = w_out_ab[j].astype(BF16), zero_bias
        else:
            glu = matmul_gated(u, conv_w_pw1[j].astype(BF16), jnp.zeros((1, 2 * D), F32), conv_b_pw1[j], row0=0, m=m,
                               tm=_pick(m, 1024, 512), tn=256, seq=t, conv=False, silu=False, out_dtype=F32)
            mix = dwconv_ln(glu, conv_w_dw[j], conv_b_dw[j], conv_ln_g[j], conv_ln_b[j], n_lat, t)
            w_o, b_o = conv_w_pw2[j].astype(BF16), conv_b_pw2[j]
        hcur, u = matmul_residual(mix, w_o, b_o, hcur, g_post1, mods[l], 2, n_lat, t,
                                  nxt=(g_pre2, mods[l], 3, 4))
        w_up = ffn_w_up[l].astype(BF16)
        act = matmul_gated(u, w_up, ffn_w_conv[l], ffn_b_conv[l], row0=0, m=n_lat,
                           tm=t, tn=256, seq=t, conv=True, silu=True, out_dtype=BF16)
        if not last:
            act_ctx = matmul_gated(u, w_up, ffn_w_conv[l], ffn_b_conv[l], row0=n_lat, m=n_ctx,
                                   tm=n_ctx, tn=256, seq=tc, conv=True, silu=True, out_dtype=BF16)
            act = jnp.concatenate([act, act_ctx], axis=0)
        nxt = None if last else (norm_gains[l + 1, 0], mods[l + 1], 0, 1)
        hcur, u = matmul_residual(act, ffn_w_down[l].astype(BF16), zero_bias, hcur, g_post2, mods[l], 5, n_lat, t,
                                  nxt=nxt)
    return hcur.reshape(b, t, D)
```

```python
import functools

import numpy as np
import jax
import jax.numpy as jnp
from jax import lax
from jax.experimental import pallas as pl
from jax.experimental.pallas import tpu as pltpu

F32 = jnp.float32
BF16 = jnp.bfloat16

D = 2048
H = D // 256
DH = 128
ROPE = 64
QL = D // 4
KVL = D // 8
DFF = (11 * D) // 4
GRID_W = 64
ROPE_BASE = 10000.0
CONV_W = 31
CONV_HALO = 16
EPS = 1e-6
ATTN_SCALE = (DH + ROPE) ** -0.5
P_FRONT = 1024
P_WIDTH = P_FRONT + 5 * H * DH
GLA_C = 128
GLA_LEVELS = 7
GLA_HB = 4
SUBLANES = 8
VMEM_LIMIT = 56 * 1024 * 1024


def _cp(*sem):
    return pltpu.CompilerParams(dimension_semantics=sem, vmem_limit_bytes=VMEM_LIMIT)


def _sigmoid(x):
    return 1.0 / (1.0 + jnp.exp(-x))


def _rms(x):
    return x * lax.rsqrt(jnp.mean(x * x, axis=-1, keepdims=True) + EPS)


def _dot(a, b):
    return jnp.dot(a, b, preferred_element_type=F32)


def _dot_nt(a, b):
    return lax.dot_general(a, b, (((1,), (1,)), ((), ())), preferred_element_type=F32)


def _dot_tn(a, b):
    return lax.dot_general(a, b, (((0,), (0,)), ((), ())), preferred_element_type=F32)


def _mod_row(i, tm, n_lat, seq):
    return jnp.where(i < n_lat // tm, i // (seq // tm), n_lat // seq)


def _mod_body(c_ref, w_ref, b_ref, o_ref):
    c = c_ref[...]
    s = (c * _sigmoid(c)).astype(BF16)
    o_ref[0] = _dot(s, w_ref[0].astype(BF16)) + b_ref[0]


def modulation(cc, w_mod, b_mod):
    depth, _, n = w_mod.shape
    rows = cc.shape[0]
    tn = 1024
    return pl.pallas_call(
        _mod_body,
        grid=(depth, n // tn),
        in_specs=[pl.BlockSpec((rows, D), lambda l, j: (0, 0)),
                  pl.BlockSpec((1, D, tn), lambda l, j: (l, 0, j)),
                  pl.BlockSpec((1, 1, tn), lambda l, j: (l, 0, j))],
        out_specs=pl.BlockSpec((1, rows, tn), lambda l, j: (l, 0, j)),
        out_shape=jax.ShapeDtypeStruct((depth, rows, n), F32),
        compiler_params=_cp("parallel", "parallel"),
        name="modulation",
    )(cc, w_mod, b_mod.reshape(depth, 1, n))


def _norm_mod_body(x_ref, g_ref, m_ref, o_ref, *, shift, scale):
    m = m_ref[0]
    y = _rms(x_ref[...]) * g_ref[...]
    o_ref[...] = (y * (1.0 + m[:, scale * D:(scale + 1) * D]) + m[:, shift * D:(shift + 1) * D]).astype(BF16)


def norm_mod(x, gain, mods, n_lat, seq, shift, scale):
    m = x.shape[0]
    tm = 256
    return pl.pallas_call(
        functools.partial(_norm_mod_body, shift=shift, scale=scale),
        grid=(m // tm,),
        in_specs=[pl.BlockSpec((tm, D), lambda i: (i, 0)),
                  pl.BlockSpec((1, D), lambda i: (0, 0)),
                  pl.BlockSpec((1, 1, 6 * D), lambda i: (_mod_row(i, tm, n_lat, seq), 0, 0))],
        out_specs=pl.BlockSpec((tm, D), lambda i: (i, 0)),
        out_shape=jax.ShapeDtypeStruct((m, D), BF16),
        compiler_params=_cp("parallel"),
        name="norm_mod",
    )(x, gain.reshape(1, D), mods)


def _mm_body(x_ref, w_ref, o_ref):
    o_ref[...] = _dot(x_ref[...], w_ref[...])


def matmul(x, w, tm, tn):
    m, k = x.shape
    n = w.shape[1]
    return pl.pallas_call(
        _mm_body,
        grid=(m // tm, n // tn),
        in_specs=[pl.BlockSpec((tm, k), lambda i, j: (i, 0)),
                  pl.BlockSpec((k, tn), lambda i, j: (0, j))],
        out_specs=pl.BlockSpec((tm, tn), lambda i, j: (i, j)),
        out_shape=jax.ShapeDtypeStruct((m, n), F32),
        compiler_params=_cp("parallel", "parallel"),
        name="in_proj",
    )(x, w)


def _mla_prep_body(p_ref, qn_ref, wq_ref, kvn_ref, wkv_ref, cos_ref, sin_ref, q_ref, k_ref, v_ref):
    p = p_ref[...]
    tm = p.shape[0]
    cos = cos_ref[...]
    sin = sin_ref[...]
    lane = lax.broadcasted_iota(jnp.int32, (tm, DH), 1)
    first_half = (lane & 31) < 16

    def rope(x):
        partner = jnp.where(first_half, pltpu.roll(x, DH - 16, 1), pltpu.roll(x, 16, 1))
        return x * cos + partner * sin

    cq = (_rms(p[:, :QL]) * qn_ref[...]).astype(BF16)
    ckv = (_rms(p[:, QL:QL + KVL]) * kvn_ref[...]).astype(BF16)
    kr = rope(p[:, QL + KVL:QL + KVL + DH]).astype(BF16)
    q = _dot(cq, wq_ref[...])
    kv = _dot(ckv, wkv_ref[...])
    for h in range(H):
        lo = 2 * DH * h
        q_ref[:, lo:lo + DH] = (q[:, lo:lo + DH] * ATTN_SCALE).astype(BF16)
        q_ref[:, lo + DH:lo + 2 * DH] = (rope(q[:, lo + DH:lo + 2 * DH]) * ATTN_SCALE).astype(BF16)
        k_ref[:, lo:lo + DH] = kv[:, DH * h:DH * (h + 1)].astype(BF16)
        k_ref[:, lo + DH:lo + 2 * DH] = kr
    v_ref[...] = kv[:, H * DH:].astype(BF16)


def mla_prep(p, q_norm, wq, kv_norm, wkv, cos, sin):
    m = p.shape[0]
    tm = 512
    row = lambda i: (i, 0)
    fixed = lambda i: (0, 0)
    return pl.pallas_call(
        _mla_prep_body,
        grid=(m // tm,),
        in_specs=[pl.BlockSpec((tm, P_FRONT), row),
                  pl.BlockSpec((1, QL), fixed),
                  pl.BlockSpec((QL, 2 * H * DH), fixed),
                  pl.BlockSpec((1, KVL), fixed),
                  pl.BlockSpec((KVL, 2 * H * DH), fixed),
                  pl.BlockSpec((tm, DH), row),
                  pl.BlockSpec((tm, DH), row)],
        out_specs=[pl.BlockSpec((tm, 2 * H * DH), row),
                   pl.BlockSpec((tm, 2 * H * DH), row),
                   pl.BlockSpec((tm, H * DH), row)],
        out_shape=[jax.ShapeDtypeStruct((m, 2 * H * DH), BF16),
                   jax.ShapeDtypeStruct((m, 2 * H * DH), BF16),
                   jax.ShapeDtypeStruct((m, H * DH), BF16)],
        compiler_params=_cp("parallel"),
        name="mla_prep",
    )(p, q_norm.reshape(1, QL), wq, kv_norm.reshape(1, KVL), wkv, cos, sin)


def _attn_lat_body(q_ref, kl_ref, vl_ref, kc_ref, vc_ref, o_ref):
    q = q_ref[...]
    sl = _dot_nt(q, kl_ref[...])
    sc = _dot_nt(q, kc_ref[...])
    mx = jnp.maximum(jnp.max(sl, axis=-1, keepdims=True), jnp.max(sc, axis=-1, keepdims=True))
    el = jnp.exp(sl - mx)
    ec = jnp.exp(sc - mx)
    den = jnp.sum(el, axis=-1, keepdims=True) + jnp.sum(ec, axis=-1, keepdims=True)
    o = _dot(el.astype(BF16), vl_ref[...]) + _dot(ec.astype(BF16), vc_ref[...])
    o_ref[...] = (o / den).astype(BF16)


def _attn_ctx_body(q_ref, kc_ref, vc_ref, o_ref):
    sc = _dot_nt(q_ref[...], kc_ref[...])
    ec = jnp.exp(sc - jnp.max(sc, axis=-1, keepdims=True))
    o = _dot(ec.astype(BF16), vc_ref[...])
    o_ref[...] = (o / jnp.sum(ec, axis=-1, keepdims=True)).astype(BF16)


def attention_latent(q, k, v, b, t, tc):
    n_lat = b * t
    tq = 512
    nq = t // tq
    cb = n_lat // tc
    return pl.pallas_call(
        _attn_lat_body,
        grid=(b, H, nq),
        in_specs=[pl.BlockSpec((tq, 2 * DH), lambda bi, h, qi: (bi * nq + qi, h)),
                  pl.BlockSpec((t, 2 * DH), lambda bi, h, qi: (bi, h)),
                  pl.BlockSpec((t, DH), lambda bi, h, qi: (bi, h)),
                  pl.BlockSpec((tc, 2 * DH), lambda bi, h, qi: (cb + bi, h)),
                  pl.BlockSpec((tc, DH), lambda bi, h, qi: (cb + bi, h))],
        out_specs=pl.BlockSpec((tq, DH), lambda bi, h, qi: (bi * nq + qi, h)),
        out_shape=jax.ShapeDtypeStruct((n_lat, H * DH), BF16),
        compiler_params=_cp("parallel", "parallel", "parallel"),
        name="attn_latent",
    )(q, k, v, k, v)


def attention_context(q, k, v, b, t, tc):
    cb = (b * t) // tc
    return pl.pallas_call(
        _attn_ctx_body,
        grid=(b, H),
        in_specs=[pl.BlockSpec((tc, 2 * DH), lambda bi, h: (cb + bi, h)),
                  pl.BlockSpec((tc, 2 * DH), lambda bi, h: (cb + bi, h)),
                  pl.BlockSpec((tc, DH), lambda bi, h: (cb + bi, h))],
        out_specs=pl.BlockSpec((tc, DH), lambda bi, h: (bi, h)),
        out_shape=jax.ShapeDtypeStruct((b * tc, H * DH), BF16),
        compiler_params=_cp("parallel", "parallel"),
        name="attn_context",
    )(q, k, v)


def _gla_tables():
    c = GLA_C
    t = np.arange(c)[:, None]
    r = np.arange(c)[None, :]
    mats = []
    for lv in range(GLA_LEVELS):
        same = (t >> (lv + 1)) == (r >> (lv + 1))
        bt = (t >> lv) & 1
        br = (r >> lv) & 1
        upper = same & (bt == 1) & (br == 1) & (r <= t)
        lower = same & (bt == 0) & (br == 0) & (r > t)
        mats.append(upper | lower)
    mats.append(r <= t)
    mats.append(r > t)
    fwd = np.stack(mats).astype(np.float32)
    bwd = fwd[:, ::-1, ::-1]
    masks = np.stack([fwd, bwd]).reshape(2, (GLA_LEVELS + 2) * c, c)
    masks = np.concatenate([masks, masks], axis=2)
    x = t ^ r
    lvl = np.where(x == 0, -1, np.floor(np.log2(np.maximum(x, 1)))).astype(np.int32)
    return masks, lvl


def _gla_chain(q, x, v, lb_raw, msk, level_is, row_bit, st_ref, slot, *, d, layer):
    c = GLA_C
    v = v.astype(BF16)
    log_sig = jnp.minimum(x, 0.0) - jnp.log1p(jnp.exp(-jnp.abs(x)))
    if layer == 0:
        g = log_sig
        kk = 1.0 / (1.0 + jnp.exp(x))
    else:
        e = jnp.exp(lb_raw - jnp.max(lb_raw, axis=0, keepdims=True))
        prob = e / jnp.sum(e, axis=0, keepdims=True)
        lb = jnp.sum(prob[1:layer + 1], axis=0, keepdims=True)
        a = jnp.log(lb)
        bb = jnp.log1p(-lb) + log_sig
        g = jnp.maximum(a, bb) + jnp.log1p(jnp.exp(-jnp.abs(a - bb)))
        kk = (1.0 - lb) / (1.0 + jnp.exp(x))

    g_hi = g.astype(BF16)
    g_lo = (g - g_hi.astype(F32)).astype(BF16)
    ex = jnp.exp(_dot(msk, jnp.concatenate([g_hi, g_lo], axis=0)))

    att = jnp.where(level_is[-1], _dot_nt(q.astype(BF16), kk.astype(BF16)), 0.0)
    for lv in range(GLA_LEVELS):
        ex_l = ex[lv * c:(lv + 1) * c]
        half = 1 << lv
        if half < SUBLANES:
            is_q = row_bit[lv] if d == 0 else jnp.logical_not(row_bit[lv])
            scaled = jnp.where(is_q, q, kk) * ex_l
            ql = jnp.where(is_q, scaled, 0.0)
            kl = jnp.where(is_q, 0.0, scaled)
        else:
            zeros = jnp.zeros((half, DH), F32)
            qparts, kparts = [], []
            for blk in range(c // half):
                rows = slice(blk * half, (blk + 1) * half)
                if (blk & 1) == 1 - d:
                    qparts.append(q[rows] * ex_l[rows])
                    kparts.append(zeros)
                else:
                    qparts.append(zeros)
                    kparts.append(kk[rows] * ex_l[rows])
            ql = jnp.concatenate(qparts, axis=0)
            kl = jnp.concatenate(kparts, axis=0)
        att = jnp.where(level_is[lv], _dot_nt(ql.astype(BF16), kl.astype(BF16)), att)

    q_in = (q * ex[GLA_LEVELS * c:(GLA_LEVELS + 1) * c]).astype(BF16)
    k_out = (kk * ex[(GLA_LEVELS + 1) * c:]).astype(BF16)
    st = st_ref[slot]
    o = _dot(att.astype(BF16), v) + _dot_nt(q_in, st.astype(BF16))
    st_ref[slot] = st * jnp.exp(jnp.sum(g, axis=0, keepdims=True)) + _dot_tn(v, k_out)
    return o


def _gla_body(qf_ref, ff_ref, vf_ref, qb_ref, fb_ref, vb_ref, lb_ref, msk_ref, lvl_ref, of_ref, ob_ref, st_ref,
              *, layer):
    c = GLA_C

    @pl.when(pl.program_id(2) == 0)
    def _():
        st_ref[...] = jnp.zeros_like(st_ref)

    lvl = lvl_ref[...]
    level_is = [lvl == lv for lv in range(GLA_LEVELS)] + [lvl == -1]
    row = lax.broadcasted_iota(jnp.int32, (c, DH), 0)
    row_bit = [((row >> lv) & 1) == 1 for lv in range(GLA_LEVELS)]
    for d, (q_ref, f_ref, v_ref, o_ref) in enumerate(((qf_ref, ff_ref, vf_ref, of_ref),
                                                      (qb_ref, fb_ref, vb_ref, ob_ref))):
        for hh in range(GLA_HB):
            cols = slice(hh * DH, (hh + 1) * DH)
            o_ref[:, cols] = _gla_chain(q_ref[:, cols], f_ref[:, cols], v_ref[:, cols], lb_ref[d][:, cols],
                                        msk_ref[d], level_is, row_bit, st_ref, d * GLA_HB + hh, d=d, layer=layer)


def gla(p, lb_raw, masks, lvl, b, t, tc, layer):
    c = GLA_C
    m = p.shape[0]
    n_ctx_chunks = tc // c
    n_lat_chunks = t // c
    n_chunks = n_ctx_chunks + n_lat_chunks
    ctx0 = (b * t) // c
    n_even = lb_raw.shape[1]
    wb = GLA_HB * DH

    def blk(bi, i, d):
        jc = i if d == 0 else n_ctx_chunks - 1 - i
        jl = i - n_ctx_chunks if d == 0 else n_chunks - 1 - i
        return jnp.where(i < n_ctx_chunks, ctx0 + bi * n_ctx_chunks + jc, bi * n_lat_chunks + jl)

    def spec(section, d):
        col0 = (P_FRONT + section * H * DH) // wb
        return pl.BlockSpec((c, wb), lambda bi, hh, i: (blk(bi, i, d), col0 + hh))

    out_spec = lambda d: pl.BlockSpec((c, wb), lambda bi, hh, i: (blk(bi, i, d), hh))
    return pl.pallas_call(
        functools.partial(_gla_body, layer=layer),
        grid=(b, H // GLA_HB, n_chunks),
        in_specs=[spec(0, 0), spec(1, 0), spec(3, 0), spec(0, 1), spec(2, 1), spec(3, 1),
                  pl.BlockSpec((2, n_even, wb), lambda bi, hh, i: (0, 0, hh)),
                  pl.BlockSpec((2, (GLA_LEVELS + 2) * c, 2 * c), lambda bi, hh, i: (0, 0, 0)),
                  pl.BlockSpec((c, c), lambda bi, hh, i: (0, 0))],
        out_specs=[out_spec(0), out_spec(1)],
        out_shape=[jax.ShapeDtypeStruct((m, H * DH), F32)] * 2,
        scratch_shapes=[pltpu.VMEM((2 * GLA_HB, DH, DH), F32)],
        compiler_params=_cp("parallel", "parallel", "arbitrary"),
        name="hgrn_scan",
    )(p, p, p, p, p, p, lb_raw, masks, lvl)


def _hg_merge_body(of_ref, ob_ref, g_ref, n_ref, y_ref):
    o = of_ref[...] + ob_ref[...]
    gate = g_ref[...]
    gain = n_ref[...]
    for h in range(H):
        sl = slice(h * DH, (h + 1) * DH)
        gh = gate[:, sl]
        y_ref[:, sl] = (_rms(o[:, sl]) * gain * (gh * _sigmoid(gh))).astype(BF16)


def hg_merge(o_f, o_b, p, hg_norm):
    m = p.shape[0]
    tm = 512
    w = H * DH
    row = lambda i: (i, 0)
    return pl.pallas_call(
        _hg_merge_body,
        grid=(m // tm,),
        in_specs=[pl.BlockSpec((tm, w), row),
                  pl.BlockSpec((tm, w), row),
                  pl.BlockSpec((tm, w), lambda i: (i, (P_WIDTH - w) // w)),
                  pl.BlockSpec((1, DH), lambda i: (0, 0))],
        out_specs=pl.BlockSpec((tm, w), row),
        out_shape=jax.ShapeDtypeStruct((m, w), BF16),
        compiler_params=_cp("parallel"),
        name="hgrn_merge",
    )(o_f, o_b, p, hg_norm.reshape(1, DH))


RES_TK = 512
RES_ROWS = 128
RES_MM_ROWS = 256


def _mm_res_body(*refs, pieces, gate, shift, scale, nk, has_bias, has_next):
    refs = list(refs)
    a_refs = [refs.pop(0) for _ in pieces]
    w_ref = refs.pop(0)
    b_ref = refs.pop(0) if has_bias else None
    x_ref, gp_ref, mc_ref = refs.pop(0), refs.pop(0), refs.pop(0)
    gn_ref, mn_ref = (refs.pop(0), refs.pop(0)) if has_next else (None, None)
    xo_ref = refs.pop(0)
    uo_ref = refs.pop(0) if has_next else None
    i = pl.program_id(0)
    k = pl.program_id(1)

    @pl.when(k == 0)
    def _():
        xo_ref[...] = jnp.zeros_like(xo_ref)

    w = w_ref[...].astype(BF16)
    for a_ref, (r0, nr, k0, nks) in zip(a_refs, pieces):
        @pl.when((i >= r0) & (i < r0 + nr) & (k >= k0) & (k < k0 + nks))
        def _():
            for r0 in range(0, xo_ref.shape[0], RES_MM_ROWS):
                xo_ref[r0:r0 + RES_MM_ROWS, :] += _dot(a_ref[r0:r0 + RES_MM_ROWS, :], w)

    @pl.when(k == nk - 1)
    def _():
        mc = mc_ref[0]
        gate_v = mc[:, gate * D:(gate + 1) * D]
        gp = gp_ref[...]
        if has_next:
            mn = mn_ref[0]
            gn = gn_ref[...]
            scale_v = 1.0 + mn[:, scale * D:(scale + 1) * D]
            shift_v = mn[:, shift * D:(shift + 1) * D]

        def rows_pass(r, carry):
            rows = pl.ds(pl.multiple_of(r * RES_ROWS, RES_ROWS), RES_ROWS)
            y = xo_ref[rows, :]
            if has_bias:
                y = y + b_ref[...]
            xn = x_ref[rows, :] + gate_v * (_rms(y) * gp)
            xo_ref[rows, :] = xn
            if has_next:
                uo_ref[rows, :] = (_rms(xn) * gn * scale_v + shift_v).astype(BF16)
            return carry

        lax.fori_loop(0, xo_ref.shape[0] // RES_ROWS, rows_pass, 0)


def matmul_residual(pieces, w, bias, x, m, g_post, mods_cur, gate, n_lat, seq, tm, nxt=None):
    tk = RES_TK
    nk = w.shape[0] // tk
    mrow = lambda i, k: (_mod_row(i, tm, n_lat, seq), 0, 0)
    row = lambda i, k: (i, 0)
    fixed = lambda i, k: (0, 0)
    once = pl.Buffered(1)

    def piece_spec(r0, nr, k0, nks):
        return pl.BlockSpec((tm, tk), lambda i, k: (jnp.clip(i - r0, 0, nr - 1), jnp.clip(k - k0, 0, nks - 1)))

    in_specs = [piece_spec(*pc[1:]) for pc in pieces] + [pl.BlockSpec((tk, D), lambda i, k: (k, 0))]
    args = [pc[0] for pc in pieces] + [w]
    if bias is not None:
        in_specs.append(pl.BlockSpec((1, D), fixed))
        args.append(bias.reshape(1, D))
    in_specs += [pl.BlockSpec((tm, D), row, pipeline_mode=once),
                 pl.BlockSpec((1, D), fixed),
                 pl.BlockSpec((1, 1, 6 * D), mrow)]
    args += [x, g_post.reshape(1, D), mods_cur]
    out_specs = [pl.BlockSpec((tm, D), row)]
    out_shape = [jax.ShapeDtypeStruct((m, D), F32)]
    shift = scale = 0
    if nxt is not None:
        g_next, mods_next, shift, scale = nxt
        in_specs += [pl.BlockSpec((1, D), fixed), pl.BlockSpec((1, 1, 6 * D), mrow)]
        args += [g_next.reshape(1, D), mods_next]
        out_specs.append(pl.BlockSpec((tm, D), row))
        out_shape.append(jax.ShapeDtypeStruct((m, D), BF16))
    outs = pl.pallas_call(
        functools.partial(_mm_res_body, pieces=tuple(pc[1:] for pc in pieces), gate=gate, shift=shift, scale=scale,
                          nk=nk, has_bias=bias is not None, has_next=nxt is not None),
        grid=(m // tm, nk),
        in_specs=in_specs,
        out_specs=out_specs,
        out_shape=out_shape,
        compiler_params=_cp("parallel", "arbitrary"),
        name="proj_residual",
    )(*args)
    return (outs[0], outs[1]) if nxt is not None else (outs[0], None)


def _mm_gated_body(*refs, seq, conv, silu):
    if conv:
        x_ref, wa_ref, wg_ref, ca_ref, cg_ref, ba_ref, bg_ref, o_ref = refs
    else:
        x_ref, wa_ref, wg_ref, ba_ref, bg_ref, o_ref = refs
    x = x_ref[...]
    za = _dot(x, wa_ref[...].astype(BF16))
    zg = _dot(x, wg_ref[...].astype(BF16))
    if conv:
        tm = za.shape[0]
        pos = lax.broadcasted_iota(jnp.int32, za.shape, 0) & (seq - 1)
        at_start = pos == 0
        at_end = pos == seq - 1

        def conv3(z, cw):
            prev = jnp.where(at_start, 0.0, pltpu.roll(z, 1, 0))
            nxt = jnp.where(at_end, 0.0, pltpu.roll(z, tm - 1, 0))
            return cw[0:1] * prev + cw[1:2] * z + cw[2:3] * nxt

        za = conv3(za, ca_ref[...])
        zg = conv3(zg, cg_ref[...])
    za = za + ba_ref[...]
    zg = zg + bg_ref[...]
    sg = _sigmoid(zg)
    o_ref[...] = (za * (zg * sg if silu else sg)).astype(o_ref.dtype)


def matmul_gated(x, w, conv_w, bias, *, row0, m, tm, tn, seq, silu, out_dtype, name):
    kdim = x.shape[1]
    n = w.shape[1] // 2
    nj = n // tn
    i0 = row0 // tm
    first = lambda i, j: (0, j)
    second = lambda i, j: (0, j + nj)
    bias = bias.reshape(1, 2 * n)
    in_specs = [pl.BlockSpec((tm, kdim), lambda i, j: (i + i0, 0)),
                pl.BlockSpec((kdim, tn), first),
                pl.BlockSpec((kdim, tn), second)]
    args = [x, w, w]
    if conv_w is not None:
        in_specs += [pl.BlockSpec((conv_w.shape[0], tn), first), pl.BlockSpec((conv_w.shape[0], tn), second)]
        args += [conv_w, conv_w]
    in_specs += [pl.BlockSpec((1, tn), first), pl.BlockSpec((1, tn), second)]
    args += [bias, bias]
    return pl.pallas_call(
        functools.partial(_mm_gated_body, seq=seq, conv=conv_w is not None, silu=silu),
        grid=(m // tm, nj),
        in_specs=in_specs,
        out_specs=pl.BlockSpec((tm, tn), lambda i, j: (i, j)),
        out_shape=jax.ShapeDtypeStruct((m, n), out_dtype),
        compiler_params=_cp("parallel", "parallel"),
        name=name,
    )(*args)


def _dwconv_ln_body(prev_ref, cur_ref, next_ref, w_ref, b_ref, g_ref, beta_ref, o_ref, ext_ref, acc_ref,
                    *, tiles_per_seq, n_lat_tiles):
    i = pl.program_id(0)
    tm = cur_ref.shape[0]
    hw = CONV_HALO
    is_lat = i < n_lat_tiles
    j = i % tiles_per_seq
    has_prev = jnp.logical_and(is_lat, j > 0).astype(F32)
    has_next = jnp.logical_and(is_lat, j < tiles_per_seq - 1).astype(F32)
    ext_ref[0:hw, :] = prev_ref[tm - hw:, :] * has_prev
    ext_ref[hw:hw + tm, :] = cur_ref[...]
    ext_ref[hw + tm:, :] = next_ref[0:hw, :] * has_next

    def lane_block(cb, carry):
        col = pl.multiple_of(cb * DH, DH)
        wv = w_ref[:, pl.ds(col, DH)]
        for r0 in range(0, tm, 128):
            acc = jnp.zeros((128, DH), F32)
            for k in range(CONV_W):
                off = r0 + k + hw - (CONV_W - 1) // 2
                acc = acc + ext_ref[off:off + 128, pl.ds(col, DH)] * wv[k:k + 1]
            acc_ref[r0:r0 + 128, pl.ds(col, DH)] = acc
        return carry

    lax.fori_loop(0, D // DH, lane_block, 0)
    hcv = acc_ref[...] + b_ref[...]
    mu = jnp.mean(hcv, axis=-1, keepdims=True)
    xc = hcv - mu
    var = jnp.mean(xc * xc, axis=-1, keepdims=True)
    y = xc * lax.rsqrt(var + EPS) * g_ref[...] + beta_ref[...]
    o_ref[...] = (y * _sigmoid(y)).astype(BF16)


def dwconv_ln(hx, w_dw, b_dw, ln_g, ln_b, n_lat, seq):
    m = hx.shape[0]
    tm = 256
    nt = m // tm
    fixed = lambda i: (0, 0)
    vec = lambda a: a.reshape(1, D)
    return pl.pallas_call(
        functools.partial(_dwconv_ln_body, tiles_per_seq=seq // tm, n_lat_tiles=n_lat // tm),
        grid=(nt,),
        in_specs=[pl.BlockSpec((tm, D), lambda i: (jnp.maximum(i - 1, 0), 0)),
                  pl.BlockSpec((tm, D), lambda i: (i, 0)),
                  pl.BlockSpec((tm, D), lambda i: (jnp.minimum(i + 1, nt - 1), 0)),
                  pl.BlockSpec((CONV_W, D), fixed),
                  pl.BlockSpec((1, D), fixed),
                  pl.BlockSpec((1, D), fixed),
                  pl.BlockSpec((1, D), fixed)],
        out_specs=pl.BlockSpec((tm, D), lambda i: (i, 0)),
        out_shape=jax.ShapeDtypeStruct((m, D), BF16),
        scratch_shapes=[pltpu.VMEM((tm + 2 * CONV_HALO, D), F32), pltpu.VMEM((tm, D), F32)],
        compiler_params=_cp("parallel"),
        name="dwconv_ln",
    )(hx, hx, hx, w_dw, vec(b_dw), vec(ln_g), vec(ln_b))


def _rope_tables(b, t, tc):
    rows = t // GRID_W
    row = jnp.repeat(jnp.arange(rows), GRID_W).astype(F32)
    col = jnp.tile(jnp.arange(GRID_W), rows).astype(F32)
    pairs = ROPE // 4
    freqs = ROPE_BASE ** (-jnp.arange(pairs, dtype=F32) / pairs)
    ar = row[:, None] * freqs
    ac = col[:, None] * freqs
    cos = jnp.concatenate([jnp.cos(ar), jnp.cos(ar), jnp.cos(ac), jnp.cos(ac), jnp.ones((t, DH - ROPE), F32)], axis=1)
    sin = jnp.concatenate([-jnp.sin(ar), jnp.sin(ar), -jnp.sin(ac), jnp.sin(ac), jnp.zeros((t, DH - ROPE), F32)], axis=1)
    cos = jnp.concatenate([jnp.tile(cos, (b, 1)), jnp.ones((b * tc, DH), F32)], axis=0)
    sin = jnp.concatenate([jnp.tile(sin, (b, 1)), jnp.zeros((b * tc, DH), F32)], axis=0)
    return cos, sin


def _pick(n, *cands):
    for cand in cands:
        if n % cand == 0:
            return cand
    raise ValueError(f"no tile for {n}")


def kernel(x, c, ctx, c_ctx, w_mod, b_mod, norm_gains, w_in_ab, mla_q_norm, w_q_up, mla_kv_norm, w_kv_up, hgrn_lb, hgrn_norm, w_out_ab, conv_w_pw1, conv_b_pw1, conv_w_dw, conv_b_dw, conv_ln_g, conv_ln_b, conv_w_pw2, conv_b_pw2, ffn_w_up, ffn_w_conv, ffn_b_conv, ffn_w_down):
    b, t, _ = x.shape
    tc = ctx.shape[1]
    depth = w_mod.shape[0]
    n_lat = b * t
    n_ctx = b * tc
    assert x.shape[2] == D and depth % 2 == 0 and b < 8
    assert t % 512 == 0 and tc % 256 == 0 and n_ctx % 512 == 0 and t % n_ctx == 0 and tc & (tc - 1) == 0
    tr = _pick(n_ctx, 1024, 512)
    lat_tiles = n_lat // tr
    ctx_tiles = n_ctx // tr

    hcur = jnp.concatenate([x.reshape(n_lat, D), ctx.reshape(n_ctx, D)], axis=0)
    cc = jnp.concatenate([c, c_ctx[None, :], jnp.zeros((7 - b, D), F32)], axis=0)
    mods = modulation(cc, w_mod, b_mod).reshape(depth, 8, 1, 6 * D)
    cos, sin = _rope_tables(b, t, tc)
    masks_np, lvl_np = _gla_tables()
    masks = jnp.asarray(masks_np, BF16)
    lvl = jnp.asarray(lvl_np)

    u = norm_mod(hcur, norm_gains[0, 0], mods[0], n_lat, t, shift=0, scale=1)
    for l in range(depth):
        last = l == depth - 1
        j = l // 2
        m = n_lat if last else n_lat + n_ctx
        row_tiles = m // tr
        g_pre1, g_post1, g_pre2, g_post2 = norm_gains[l]
        if l % 2 == 0:
            w_in = w_in_ab[j]
            w_in_p = jnp.concatenate([w_in[:, :QL + KVL + ROPE], jnp.zeros((D, P_FRONT - QL - KVL - ROPE), F32),
                                      w_in[:, QL + KVL + ROPE:]], axis=1).astype(BF16)
            wq = jnp.pad(w_q_up[j].reshape(QL, H, DH + ROPE), ((0, 0), (0, 0), (0, DH - ROPE)))
            wq = wq.reshape(QL, 2 * H * DH).astype(BF16)
            wkv = w_kv_up[j].reshape(KVL, H, 2 * DH)
            wkv = jnp.concatenate([wkv[:, :, :DH].reshape(KVL, H * DH), wkv[:, :, DH:].reshape(KVL, H * DH)], axis=1)
            p = matmul(u, w_in_p, _pick(m, 1024, 512), 512)
            q, k, v = mla_prep(p, mla_q_norm[j], wq, mla_kv_norm[j], wkv.astype(BF16), cos, sin)
            att_lat = attention_latent(q, k, v, b, t, tc)
            att_ctx = attention_context(q, k, v, b, t, tc)
            o_f, o_b = gla(p, hgrn_lb, masks, lvl, b, t, tc, j)
            hg = hg_merge(o_f, o_b, p, hgrn_norm[j])
            nka = (H * DH) // RES_TK
            pieces = [(att_lat, 0, lat_tiles, 0, nka), (att_ctx, lat_tiles, ctx_tiles, 0, nka),
                      (hg, 0, row_tiles, nka, nka)]
            w_o, b_o = w_out_ab[j], None
        else:
            glu = matmul_gated(u, conv_w_pw1[j], None, conv_b_pw1[j], row0=0, m=m, tm=_pick(m, 1024, 512), tn=256,
                               seq=t, silu=False, out_dtype=F32, name="conformer_glu")
            mix = dwconv_ln(glu, conv_w_dw[j], conv_b_dw[j], conv_ln_g[j], conv_ln_b[j], n_lat, t)
            pieces = [(mix, 0, row_tiles, 0, D // RES_TK)]
            w_o, b_o = conv_w_pw2[j], conv_b_pw2[j]
        hcur, u = matmul_residual(pieces, w_o, b_o, hcur, m, g_post1, mods[l], 2, n_lat, t, tr,
                                  nxt=(g_pre2, mods[l], 3, 4))
        nkf = DFF // RES_TK
        act = matmul_gated(u, ffn_w_up[l], ffn_w_conv[l], ffn_b_conv[l], row0=0, m=n_lat, tm=t, tn=256, seq=t,
                           silu=True, out_dtype=BF16, name="ffn_up_latent")
        pieces = [(act, 0, lat_tiles, 0, nkf)]
        if not last:
            act_ctx = matmul_gated(u, ffn_w_up[l], ffn_w_conv[l], ffn_b_conv[l], row0=n_lat, m=n_ctx, tm=n_ctx,
                                   tn=256, seq=tc, silu=True, out_dtype=BF16, name="ffn_up_context")
            pieces.append((act_ctx, lat_tiles, ctx_tiles, 0, nkf))
        nxt = None if last else (norm_gains[l + 1, 0], mods[l + 1], 0, 1)
        hcur, u = matmul_residual(pieces, ffn_w_down[l], None, hcur, m, g_post2, mods[l], 5, n_lat, t, tr, nxt=nxt)
    return hcur.reshape(b, t, D)
```

```python
import functools

import numpy as np
import jax
import jax.numpy as jnp
from jax import lax
from jax.experimental import pallas as pl
from jax.experimental.pallas import tpu as pltpu

F32 = jnp.float32
BF16 = jnp.bfloat16

D = 2048
H = D // 256
DH = 128
ROPE = 64
QL = D // 4
KVL = D // 8
DFF = (11 * D) // 4
GRID_W = 64
ROPE_BASE = 10000.0
CONV_W = 31
CONV_HALO = 16
EPS = 1e-6
ATTN_SCALE = (DH + ROPE) ** -0.5
P_FRONT = 1024
P_WIDTH = P_FRONT + 5 * H * DH
GLA_C = 128
GLA_LEVELS = 7
GLA_HB = 4
SUBLANES = 8
VMEM_LIMIT = 56 * 1024 * 1024


def _cp(*sem):
    return pltpu.CompilerParams(dimension_semantics=sem, vmem_limit_bytes=VMEM_LIMIT)


def _sigmoid(x):
    return 1.0 / (1.0 + jnp.exp(-x))


def _rms(x):
    return x * lax.rsqrt(jnp.mean(x * x, axis=-1, keepdims=True) + EPS)


def _dot(a, b):
    return jnp.dot(a, b, preferred_element_type=F32)


def _dot_nt(a, b):
    return lax.dot_general(a, b, (((1,), (1,)), ((), ())), preferred_element_type=F32)


def _dot_tn(a, b):
    return lax.dot_general(a, b, (((0,), (0,)), ((), ())), preferred_element_type=F32)


def _mod_row(i, tm, n_lat, seq):
    return jnp.where(i < n_lat // tm, i // (seq // tm), n_lat // seq)


def _mod_body(c_ref, w_ref, b_ref, o_ref):
    c = c_ref[...]
    s = (c * _sigmoid(c)).astype(BF16)
    o_ref[0] = _dot(s, w_ref[0].astype(BF16)) + b_ref[0]


def modulation(cc, w_mod, b_mod):
    depth, _, n = w_mod.shape
    rows = cc.shape[0]
    tn = 1024
    return pl.pallas_call(
        _mod_body,
        grid=(depth, n // tn),
        in_specs=[pl.BlockSpec((rows, D), lambda l, j: (0, 0)),
                  pl.BlockSpec((1, D, tn), lambda l, j: (l, 0, j)),
                  pl.BlockSpec((1, 1, tn), lambda l, j: (l, 0, j))],
        out_specs=pl.BlockSpec((1, rows, tn), lambda l, j: (l, 0, j)),
        out_shape=jax.ShapeDtypeStruct((depth, rows, n), F32),
        compiler_params=_cp("parallel", "parallel"),
        name="modulation",
    )(cc, w_mod, b_mod.reshape(depth, 1, n))


def _norm_mod_body(x_ref, g_ref, m_ref, o_ref, *, shift, scale):
    m = m_ref[0]
    y = _rms(x_ref[...]) * g_ref[...]
    o_ref[...] = (y * (1.0 + m[:, scale * D:(scale + 1) * D]) + m[:, shift * D:(shift + 1) * D]).astype(BF16)


def norm_mod(x, gain, mods, n_lat, seq, shift, scale):
    m = x.shape[0]
    tm = 256
    return pl.pallas_call(
        functools.partial(_norm_mod_body, shift=shift, scale=scale),
        grid=(m // tm,),
        in_specs=[pl.BlockSpec((tm, D), lambda i: (i, 0)),
                  pl.BlockSpec((1, D), lambda i: (0, 0)),
                  pl.BlockSpec((1, 1, 6 * D), lambda i: (_mod_row(i, tm, n_lat, seq), 0, 0))],
        out_specs=pl.BlockSpec((tm, D), lambda i: (i, 0)),
        out_shape=jax.ShapeDtypeStruct((m, D), BF16),
        compiler_params=_cp("parallel"),
        name="norm_mod",
    )(x, gain.reshape(1, D), mods)


def _mm_body(x_ref, w_ref, o_ref):
    o_ref[...] = _dot(x_ref[...], w_ref[...])


def matmul(x, w, tm, tn):
    m, k = x.shape
    n = w.shape[1]
    return pl.pallas_call(
        _mm_body,
        grid=(m // tm, n // tn),
        in_specs=[pl.BlockSpec((tm, k), lambda i, j: (i, 0)),
                  pl.BlockSpec((k, tn), lambda i, j: (0, j))],
        out_specs=pl.BlockSpec((tm, tn), lambda i, j: (i, j)),
        out_shape=jax.ShapeDtypeStruct((m, n), F32),
        compiler_params=_cp("parallel", "parallel"),
        name="in_proj",
    )(x, w)


def _mla_prep_body(p_ref, qn_ref, wq_ref, kvn_ref, wkv_ref, cos_ref, sin_ref, q_ref, k_ref, v_ref):
    p = p_ref[...]
    tm = p.shape[0]
    cos = cos_ref[...]
    sin = sin_ref[...]
    lane = lax.broadcasted_iota(jnp.int32, (tm, DH), 1)
    first_half = (lane & 31) < 16

    def rope(x):
        partner = jnp.where(first_half, pltpu.roll(x, DH - 16, 1), pltpu.roll(x, 16, 1))
        return x * cos + partner * sin

    cq = (_rms(p[:, :QL]) * qn_ref[...]).astype(BF16)
    ckv = (_rms(p[:, QL:QL + KVL]) * kvn_ref[...]).astype(BF16)
    kr = rope(p[:, QL + KVL:QL + KVL + DH]).astype(BF16)
    q = _dot(cq, wq_ref[...])
    kv = _dot(ckv, wkv_ref[...])
    for h in range(H):
        lo = 2 * DH * h
        q_ref[:, lo:lo + DH] = (q[:, lo:lo + DH] * ATTN_SCALE).astype(BF16)
        q_ref[:, lo + DH:lo + 2 * DH] = (rope(q[:, lo + DH:lo + 2 * DH]) * ATTN_SCALE).astype(BF16)
        k_ref[:, lo:lo + DH] = kv[:, DH * h:DH * (h + 1)].astype(BF16)
        k_ref[:, lo + DH:lo + 2 * DH] = kr
    v_ref[...] = kv[:, H * DH:].astype(BF16)


def mla_prep(p, q_norm, wq, kv_norm, wkv, cos, sin):
    m = p.shape[0]
    tm = 512
    row = lambda i: (i, 0)
    fixed = lambda i: (0, 0)
    return pl.pallas_call(
        _mla_prep_body,
        grid=(m // tm,),
        in_specs=[pl.BlockSpec((tm, P_FRONT), row),
                  pl.BlockSpec((1, QL), fixed),
                  pl.BlockSpec((QL, 2 * H * DH), fixed),
                  pl.BlockSpec((1, KVL), fixed),
                  pl.BlockSpec((KVL, 2 * H * DH), fixed),
                  pl.BlockSpec((tm, DH), row),
                  pl.BlockSpec((tm, DH), row)],
        out_specs=[pl.BlockSpec((tm, 2 * H * DH), row),
                   pl.BlockSpec((tm, 2 * H * DH), row),
                   pl.BlockSpec((tm, H * DH), row)],
        out_shape=[jax.ShapeDtypeStruct((m, 2 * H * DH), BF16),
                   jax.ShapeDtypeStruct((m, 2 * H * DH), BF16),
                   jax.ShapeDtypeStruct((m, H * DH), BF16)],
        compiler_params=_cp("parallel"),
        name="mla_prep",
    )(p, q_norm.reshape(1, QL), wq, kv_norm.reshape(1, KVL), wkv, cos, sin)


def _attn_lat_body(q_ref, kl_ref, vl_ref, kc_ref, vc_ref, o_ref):
    q = q_ref[...]
    sl = _dot_nt(q, kl_ref[...])
    sc = _dot_nt(q, kc_ref[...])
    mx = jnp.maximum(jnp.max(sl, axis=-1, keepdims=True), jnp.max(sc, axis=-1, keepdims=True))
    el = jnp.exp(sl - mx)
    ec = jnp.exp(sc - mx)
    den = jnp.sum(el, axis=-1, keepdims=True) + jnp.sum(ec, axis=-1, keepdims=True)
    o = _dot(el.astype(BF16), vl_ref[...]) + _dot(ec.astype(BF16), vc_ref[...])
    o_ref[...] = (o / den).astype(BF16)


def _attn_ctx_body(q_ref, kc_ref, vc_ref, o_ref):
    sc = _dot_nt(q_ref[...], kc_ref[...])
    ec = jnp.exp(sc - jnp.max(sc, axis=-1, keepdims=True))
    o = _dot(ec.astype(BF16), vc_ref[...])
    o_ref[...] = (o / jnp.sum(ec, axis=-1, keepdims=True)).astype(BF16)


def attention_latent(q, k, v, b, t, tc):
    n_lat = b * t
    tq = 512
    nq = t // tq
    cb = n_lat // tc
    return pl.pallas_call(
        _attn_lat_body,
        grid=(b, H, nq),
        in_specs=[pl.BlockSpec((tq, 2 * DH), lambda bi, h, qi: (bi * nq + qi, h)),
                  pl.BlockSpec((t, 2 * DH), lambda bi, h, qi: (bi, h)),
                  pl.BlockSpec((t, DH), lambda bi, h, qi: (bi, h)),
                  pl.BlockSpec((tc, 2 * DH), lambda bi, h, qi: (cb + bi, h)),
                  pl.BlockSpec((tc, DH), lambda bi, h, qi: (cb + bi, h))],
        out_specs=pl.BlockSpec((tq, DH), lambda bi, h, qi: (bi * nq + qi, h)),
        out_shape=jax.ShapeDtypeStruct((n_lat, H * DH), BF16),
        compiler_params=_cp("parallel", "parallel", "parallel"),
        name="attn_latent",
    )(q, k, v, k, v)


def attention_context(q, k, v, b, t, tc):
    cb = (b * t) // tc
    return pl.pallas_call(
        _attn_ctx_body,
        grid=(b, H),
        in_specs=[pl.BlockSpec((tc, 2 * DH), lambda bi, h: (cb + bi, h)),
                  pl.BlockSpec((tc, 2 * DH), lambda bi, h: (cb + bi, h)),
                  pl.BlockSpec((tc, DH), lambda bi, h: (cb + bi, h))],
        out_specs=pl.BlockSpec((tc, DH), lambda bi, h: (bi, h)),
        out_shape=jax.ShapeDtypeStruct((b * tc, H * DH), BF16),
        compiler_params=_cp("parallel", "parallel"),
        name="attn_context",
    )(q, k, v)


def _gla_tables():
    c = GLA_C
    t = np.arange(c)[:, None]
    r = np.arange(c)[None, :]
    mats = []
    for lv in range(GLA_LEVELS):
        same = (t >> (lv + 1)) == (r >> (lv + 1))
        bt = (t >> lv) & 1
        br = (r >> lv) & 1
        upper = same & (bt == 1) & (br == 1) & (r <= t)
        lower = same & (bt == 0) & (br == 0) & (r > t)
        mats.append(upper | lower)
    mats.append(r <= t)
    mats.append(r > t)
    fwd = np.stack(mats).astype(np.float32)
    bwd = fwd[:, ::-1, ::-1]
    masks = np.stack([fwd, bwd]).reshape(2, (GLA_LEVELS + 2) * c, c)
    masks = np.concatenate([masks, masks], axis=2)
    x = t ^ r
    lvl = np.where(x == 0, -1, np.floor(np.log2(np.maximum(x, 1)))).astype(np.int32)
    return masks, lvl


def _gla_chain(q, x, v, lb_raw, msk, level_is, row_bit, st_ref, slot, *, d, layer):
    c = GLA_C
    v = v.astype(BF16)
    log_sig = jnp.minimum(x, 0.0) - jnp.log1p(jnp.exp(-jnp.abs(x)))
    if layer == 0:
        g = log_sig
        kk = 1.0 / (1.0 + jnp.exp(x))
    else:
        e = jnp.exp(lb_raw - jnp.max(lb_raw, axis=0, keepdims=True))
        prob = e / jnp.sum(e, axis=0, keepdims=True)
        lb = jnp.sum(prob[1:layer + 1], axis=0, keepdims=True)
        a = jnp.log(lb)
        bb = jnp.log1p(-lb) + log_sig
        g = jnp.maximum(a, bb) + jnp.log1p(jnp.exp(-jnp.abs(a - bb)))
        kk = (1.0 - lb) / (1.0 + jnp.exp(x))

    g_hi = g.astype(BF16)
    g_lo = (g - g_hi.astype(F32)).astype(BF16)
    ex = jnp.exp(_dot(msk, jnp.concatenate([g_hi, g_lo], axis=0)))

    att = jnp.where(level_is[-1], _dot_nt(q.astype(BF16), kk.astype(BF16)), 0.0)
    for lv in range(GLA_LEVELS):
        ex_l = ex[lv * c:(lv + 1) * c]
        half = 1 << lv
        if half < SUBLANES:
            is_q = row_bit[lv] if d == 0 else jnp.logical_not(row_bit[lv])
            scaled = jnp.where(is_q, q, kk) * ex_l
            ql = jnp.where(is_q, scaled, 0.0)
            kl = jnp.where(is_q, 0.0, scaled)
        else:
            zeros = jnp.zeros((half, DH), F32)
            qparts, kparts = [], []
            for blk in range(c // half):
                rows = slice(blk * half, (blk + 1) * half)
                if (blk & 1) == 1 - d:
                    qparts.append(q[rows] * ex_l[rows])
                    kparts.append(zeros)
                else:
                    qparts.append(zeros)
                    kparts.append(kk[rows] * ex_l[rows])
            ql = jnp.concatenate(qparts, axis=0)
            kl = jnp.concatenate(kparts, axis=0)
        att = jnp.where(level_is[lv], _dot_nt(ql.astype(BF16), kl.astype(BF16)), att)

    q_in = (q * ex[GLA_LEVELS * c:(GLA_LEVELS + 1) * c]).astype(BF16)
    k_out = (kk * ex[(GLA_LEVELS + 1) * c:]).astype(BF16)
    st = st_ref[slot]
    o = _dot(att.astype(BF16), v) + _dot_nt(q_in, st.astype(BF16))
    st_ref[slot] = st * jnp.exp(jnp.sum(g, axis=0, keepdims=True)) + _dot_tn(v, k_out)
    return o


def _gla_body(qf_ref, ff_ref, vf_ref, qb_ref, fb_ref, vb_ref, lb_ref, msk_ref, lvl_ref, of_ref, ob_ref, st_ref,
              *, layer):
    c = GLA_C

    @pl.when(pl.program_id(2) == 0)
    def _():
        st_ref[...] = jnp.zeros_like(st_ref)

    lvl = lvl_ref[...]
    level_is = [lvl == lv for lv in range(GLA_LEVELS)] + [lvl == -1]
    row = lax.broadcasted_iota(jnp.int32, (c, DH), 0)
    row_bit = [((row >> lv) & 1) == 1 for lv in range(GLA_LEVELS)]
    for d, (q_ref, f_ref, v_ref, o_ref) in enumerate(((qf_ref, ff_ref, vf_ref, of_ref),
                                                      (qb_ref, fb_ref, vb_ref, ob_ref))):
        for hh in range(GLA_HB):
            cols = slice(hh * DH, (hh + 1) * DH)
            o_ref[:, cols] = _gla_chain(q_ref[:, cols], f_ref[:, cols], v_ref[:, cols], lb_ref[d][:, cols],
                                        msk_ref[d], level_is, row_bit, st_ref, d * GLA_HB + hh, d=d, layer=layer)


def gla(p, lb_raw, masks, lvl, b, t, tc, layer):
    c = GLA_C
    m = p.shape[0]
    n_ctx_chunks = tc // c
    n_lat_chunks = t // c
    n_chunks = n_ctx_chunks + n_lat_chunks
    ctx0 = (b * t) // c
    n_even = lb_raw.shape[1]
    wb = GLA_HB * DH

    def blk(bi, i, d):
        jc = i if d == 0 else n_ctx_chunks - 1 - i
        jl = i - n_ctx_chunks if d == 0 else n_chunks - 1 - i
        return jnp.where(i < n_ctx_chunks, ctx0 + bi * n_ctx_chunks + jc, bi * n_lat_chunks + jl)

    def spec(section, d):
        col0 = (P_FRONT + section * H * DH) // wb
        return pl.BlockSpec((c, wb), lambda bi, hh, i: (blk(bi, i, d), col0 + hh))

    out_spec = lambda d: pl.BlockSpec((c, wb), lambda bi, hh, i: (blk(bi, i, d), hh))
    return pl.pallas_call(
        functools.partial(_gla_body, layer=layer),
        grid=(b, H // GLA_HB, n_chunks),
        in_specs=[spec(0, 0), spec(1, 0), spec(3, 0), spec(0, 1), spec(2, 1), spec(3, 1),
                  pl.BlockSpec((2, n_even, wb), lambda bi, hh, i: (0, 0, hh)),
                  pl.BlockSpec((2, (GLA_LEVELS + 2) * c, 2 * c), lambda bi, hh, i: (0, 0, 0)),
                  pl.BlockSpec((c, c), lambda bi, hh, i: (0, 0))],
        out_specs=[out_spec(0), out_spec(1)],
        out_shape=[jax.ShapeDtypeStruct((m, H * DH), F32)] * 2,
        scratch_shapes=[pltpu.VMEM((2 * GLA_HB, DH, DH), F32)],
        compiler_params=_cp("parallel", "parallel", "arbitrary"),
        name="hgrn_scan",
    )(p, p, p, p, p, p, lb_raw, masks, lvl)


def _hg_merge_body(of_ref, ob_ref, g_ref, n_ref, y_ref):
    o = of_ref[...] + ob_ref[...]
    gate = g_ref[...]
    gain = n_ref[...]
    for h in range(H):
        sl = slice(h * DH, (h + 1) * DH)
        gh = gate[:, sl]
        y_ref[:, sl] = (_rms(o[:, sl]) * gain * (gh * _sigmoid(gh))).astype(BF16)


def hg_merge(o_f, o_b, p, hg_norm):
    m = p.shape[0]
    tm = 512
    w = H * DH
    row = lambda i: (i, 0)
    return pl.pallas_call(
        _hg_merge_body,
        grid=(m // tm,),
        in_specs=[pl.BlockSpec((tm, w), row),
                  pl.BlockSpec((tm, w), row),
                  pl.BlockSpec((tm, w), lambda i: (i, (P_WIDTH - w) // w)),
                  pl.BlockSpec((1, DH), lambda i: (0, 0))],
        out_specs=pl.BlockSpec((tm, w), row),
        out_shape=jax.ShapeDtypeStruct((m, w), BF16),
        compiler_params=_cp("parallel"),
        name="hgrn_merge",
    )(o_f, o_b, p, hg_norm.reshape(1, DH))


RES_TN = 256
RES_ROWS = 128
SINGLE_BUFFER_BYTES = 4 * 1024 * 1024


def _mm_res_body(*refs, groups, gate, shift, scale, nj, has_bias, has_next):
    refs = list(refs)
    a_refs = [[refs.pop(0) for _ in pieces] for _, pieces in groups]
    w_ref = refs.pop(0)
    b_ref = refs.pop(0) if has_bias else None
    x_ref, gp_ref, mc_ref = refs.pop(0), refs.pop(0), refs.pop(0)
    gn_ref, mn_ref = (refs.pop(0), refs.pop(0)) if has_next else (None, None)
    xo_ref = refs.pop(0)
    uo_ref = refs.pop(0) if has_next else None
    i = pl.program_id(0)
    j = pl.program_id(1)
    tn = w_ref.shape[1]
    cols = pl.ds(pl.multiple_of(j * tn, tn), tn)

    w = w_ref[...].astype(BF16)
    koff = 0
    for gi, ((kw, pieces), piece_refs) in enumerate(zip(groups, a_refs)):
        wk = w[koff:koff + kw]
        koff += kw
        for a_ref, (r0, nr) in zip(piece_refs, pieces):
            def project(a_ref=a_ref, wk=wk, gi=gi):
                part = _dot(a_ref[...], wk)
                if gi == 0:
                    xo_ref[:, cols] = part
                else:
                    xo_ref[:, cols] += part

            if len(pieces) == 1:
                project()
            else:
                pl.when((i >= r0) & (i < r0 + nr))(project)

    @pl.when(j == nj - 1)
    def _():
        mc = mc_ref[0]
        gate_v = mc[:, gate * D:(gate + 1) * D]
        gp = gp_ref[...]
        if has_next:
            mn = mn_ref[0]
            gn = gn_ref[...]
            scale_v = 1.0 + mn[:, scale * D:(scale + 1) * D]
            shift_v = mn[:, shift * D:(shift + 1) * D]

        def rows_pass(r, carry):
            rows = pl.ds(pl.multiple_of(r * RES_ROWS, RES_ROWS), RES_ROWS)
            y = xo_ref[rows, :]
            if has_bias:
                y = y + b_ref[...]
            xn = x_ref[rows, :] + gate_v * (_rms(y) * gp)
            xo_ref[rows, :] = xn
            if has_next:
                uo_ref[rows, :] = (_rms(xn) * gn * scale_v + shift_v).astype(BF16)
            return carry

        lax.fori_loop(0, xo_ref.shape[0] // RES_ROWS, rows_pass, 0)


def matmul_residual(groups, w, layer, bias, x, m, g_post, mods_cur, gate, n_lat, seq, tm, nxt=None):
    tn = RES_TN
    nj = D // tn
    kdim = w.shape[1]
    mrow = lambda i, j: (_mod_row(i, tm, n_lat, seq), 0, 0)
    row = lambda i, j: (i, 0)
    fixed = lambda i, j: (0, 0)
    once = pl.Buffered(1)

    def piece_spec(arr, r0, nr):
        mode = {"pipeline_mode": once} if tm * arr.shape[1] * arr.dtype.itemsize >= SINGLE_BUFFER_BYTES else {}
        return pl.BlockSpec((tm, arr.shape[1]), lambda i, j: (jnp.clip(i - r0, 0, nr - 1), 0), **mode)

    in_specs = [piece_spec(*pc) for g in groups for pc in g]
    in_specs.append(pl.BlockSpec((None, kdim, tn), lambda i, j: (layer, 0, j)))
    args = [pc[0] for g in groups for pc in g] + [w]
    if bias is not None:
        in_specs.append(pl.BlockSpec((1, D), fixed))
        args.append(bias.reshape(1, D))
    in_specs += [pl.BlockSpec((tm, D), row, pipeline_mode=once),
                 pl.BlockSpec((1, D), fixed),
                 pl.BlockSpec((1, 1, 6 * D), mrow)]
    args += [x, g_post.reshape(1, D), mods_cur]
    out_specs = [pl.BlockSpec((tm, D), row, pipeline_mode=once)]
    out_shape = [jax.ShapeDtypeStruct((m, D), F32)]
    shift = scale = 0
    if nxt is not None:
        g_next, mods_next, shift, scale = nxt
        in_specs += [pl.BlockSpec((1, D), fixed), pl.BlockSpec((1, 1, 6 * D), mrow)]
        args += [g_next.reshape(1, D), mods_next]
        out_specs.append(pl.BlockSpec((tm, D), row, pipeline_mode=once))
        out_shape.append(jax.ShapeDtypeStruct((m, D), BF16))
    group_meta = tuple((g[0][0].shape[1], tuple(pc[1:] for pc in g)) for g in groups)
    assert sum(kw for kw, _ in group_meta) == kdim
    outs = pl.pallas_call(
        functools.partial(_mm_res_body, groups=group_meta, gate=gate, shift=shift, scale=scale,
                          nj=nj, has_bias=bias is not None, has_next=nxt is not None),
        grid=(m // tm, nj),
        in_specs=in_specs,
        out_specs=out_specs,
        out_shape=out_shape,
        compiler_params=_cp("parallel", "arbitrary"),
        name="proj_residual",
    )(*args)
    return (outs[0], outs[1]) if nxt is not None else (outs[0], None)


GATED_HALO = 16


def _mm_gated_body(*refs, conv, silu, lat_tiles, seq_lat, seq_ctx):
    if conv:
        x_ref, xp_ref, xn_ref, wa_ref, wg_ref, ca_ref, cg_ref, ba_ref, bg_ref, o_ref = refs
    else:
        x_ref, wa_ref, wg_ref, ba_ref, bg_ref, o_ref = refs
    x = x_ref[...]
    wa = wa_ref[...].astype(BF16)
    wg = wg_ref[...].astype(BF16)
    za = _dot(x, wa)
    zg = _dot(x, wg)
    if conv:
        i = pl.program_id(0)
        tm, tn = za.shape
        halo = jnp.concatenate([xp_ref[...], xn_ref[...]], axis=0)
        ha = _dot(halo, wa)
        hg = _dot(halo, wg)
        seq_m1 = jnp.where(i < lat_tiles, seq_lat - 1, seq_ctx - 1)
        pos = (lax.broadcasted_iota(jnp.int32, za.shape, 0) + i * tm) & seq_m1
        at_start = pos == 0
        at_end = pos == seq_m1
        sub = lax.broadcasted_iota(jnp.int32, (SUBLANES, tn), 0)

        def conv3(z, h, cw):
            up = pltpu.roll(z, 1, 0)
            up = jnp.concatenate([jnp.where(sub == 0, h[GATED_HALO - 1:GATED_HALO], up[:SUBLANES]), up[SUBLANES:]],
                                 axis=0)
            dn = pltpu.roll(z, tm - 1, 0)
            dn = jnp.concatenate([dn[:tm - SUBLANES],
                                  jnp.where(sub == SUBLANES - 1, h[GATED_HALO:GATED_HALO + 1], dn[tm - SUBLANES:])],
                                 axis=0)
            return cw[0:1] * jnp.where(at_start, 0.0, up) + cw[1:2] * z + cw[2:3] * jnp.where(at_end, 0.0, dn)

        za = conv3(za, ha, ca_ref[...])
        zg = conv3(zg, hg, cg_ref[...])
    za = za + ba_ref[...]
    zg = zg + bg_ref[...]
    sg = _sigmoid(zg)
    o_ref[...] = (za * (zg * sg if silu else sg)).astype(o_ref.dtype)


def matmul_gated(x, w, layer, conv_w, bias, *, m, tm, tn, n_lat, seq_lat, seq_ctx, silu, out_dtype, name):
    kdim = x.shape[1]
    n = w.shape[2] // 2
    nj = n // tn
    first = lambda i, j: (layer, 0, j)
    second = lambda i, j: (layer, 0, j + nj)
    bias = bias.reshape(bias.shape[0], 1, 2 * n)
    in_specs = [pl.BlockSpec((tm, kdim), lambda i, j: (i, 0))]
    args = [x]
    if conv_w is not None:
        hb = tm // GATED_HALO
        last_hb = x.shape[0] // GATED_HALO - 1
        in_specs += [pl.BlockSpec((GATED_HALO, kdim), lambda i, j: (jnp.maximum(i * hb - 1, 0), 0)),
                     pl.BlockSpec((GATED_HALO, kdim), lambda i, j: (jnp.minimum((i + 1) * hb, last_hb), 0))]
        args += [x, x]
    in_specs += [pl.BlockSpec((None, kdim, tn), first), pl.BlockSpec((None, kdim, tn), second)]
    args += [w, w]
    if conv_w is not None:
        in_specs += [pl.BlockSpec((None, conv_w.shape[1], tn), first),
                     pl.BlockSpec((None, conv_w.shape[1], tn), second)]
        args += [conv_w, conv_w]
    in_specs += [pl.BlockSpec((None, 1, tn), first), pl.BlockSpec((None, 1, tn), second)]
    args += [bias, bias]
    return pl.pallas_call(
        functools.partial(_mm_gated_body, conv=conv_w is not None, silu=silu, lat_tiles=n_lat // tm,
                          seq_lat=seq_lat, seq_ctx=seq_ctx),
        grid=(m // tm, nj),
        in_specs=in_specs,
        out_specs=pl.BlockSpec((tm, tn), lambda i, j: (i, j)),
        out_shape=jax.ShapeDtypeStruct((m, n), out_dtype),
        compiler_params=_cp("parallel", "parallel"),
        name=name,
    )(*args)


def _dwconv_ln_body(prev_ref, cur_ref, next_ref, w_ref, b_ref, g_ref, beta_ref, o_ref, ext_ref, acc_ref,
                    *, tiles_per_seq, n_lat_tiles):
    i = pl.program_id(0)
    tm = cur_ref.shape[0]
    hw = CONV_HALO
    is_lat = i < n_lat_tiles
    j = i % tiles_per_seq
    has_prev = jnp.logical_and(is_lat, j > 0).astype(F32)
    has_next = jnp.logical_and(is_lat, j < tiles_per_seq - 1).astype(F32)
    ext_ref[0:hw, :] = prev_ref[tm - hw:, :] * has_prev
    ext_ref[hw:hw + tm, :] = cur_ref[...]
    ext_ref[hw + tm:, :] = next_ref[0:hw, :] * has_next

    def lane_block(cb, carry):
        col = pl.multiple_of(cb * DH, DH)
        wv = w_ref[:, pl.ds(col, DH)]
        for r0 in range(0, tm, 128):
            acc = jnp.zeros((128, DH), F32)
            for k in range(CONV_W):
                off = r0 + k + hw - (CONV_W - 1) // 2
                acc = acc + ext_ref[off:off + 128, pl.ds(col, DH)] * wv[k:k + 1]
            acc_ref[r0:r0 + 128, pl.ds(col, DH)] = acc
        return carry

    lax.fori_loop(0, D // DH, lane_block, 0)
    hcv = acc_ref[...] + b_ref[...]
    mu = jnp.mean(hcv, axis=-1, keepdims=True)
    xc = hcv - mu
    var = jnp.mean(xc * xc, axis=-1, keepdims=True)
    y = xc * lax.rsqrt(var + EPS) * g_ref[...] + beta_ref[...]
    o_ref[...] = (y * _sigmoid(y)).astype(BF16)


def dwconv_ln(hx, w_dw, b_dw, ln_g, ln_b, n_lat, seq):
    m = hx.shape[0]
    tm = 256
    nt = m // tm
    fixed = lambda i: (0, 0)
    vec = lambda a: a.reshape(1, D)
    return pl.pallas_call(
        functools.partial(_dwconv_ln_body, tiles_per_seq=seq // tm, n_lat_tiles=n_lat // tm),
        grid=(nt,),
        in_specs=[pl.BlockSpec((tm, D), lambda i: (jnp.maximum(i - 1, 0), 0)),
                  pl.BlockSpec((tm, D), lambda i: (i, 0)),
                  pl.BlockSpec((tm, D), lambda i: (jnp.minimum(i + 1, nt - 1), 0)),
                  pl.BlockSpec((CONV_W, D), fixed),
                  pl.BlockSpec((1, D), fixed),
                  pl.BlockSpec((1, D), fixed),
                  pl.BlockSpec((1, D), fixed)],
        out_specs=pl.BlockSpec((tm, D), lambda i: (i, 0)),
        out_shape=jax.ShapeDtypeStruct((m, D), BF16),
        scratch_shapes=[pltpu.VMEM((tm + 2 * CONV_HALO, D), F32), pltpu.VMEM((tm, D), F32)],
        compiler_params=_cp("parallel"),
        name="dwconv_ln",
    )(hx, hx, hx, w_dw, vec(b_dw), vec(ln_g), vec(ln_b))


def _rope_tables(b, t, tc):
    rows = t // GRID_W
    row = jnp.repeat(jnp.arange(rows), GRID_W).astype(F32)
    col = jnp.tile(jnp.arange(GRID_W), rows).astype(F32)
    pairs = ROPE // 4
    freqs = ROPE_BASE ** (-jnp.arange(pairs, dtype=F32) / pairs)
    ar = row[:, None] * freqs
    ac = col[:, None] * freqs
    cos = jnp.concatenate([jnp.cos(ar), jnp.cos(ar), jnp.cos(ac), jnp.cos(ac), jnp.ones((t, DH - ROPE), F32)], axis=1)
    sin = jnp.concatenate([-jnp.sin(ar), jnp.sin(ar), -jnp.sin(ac), jnp.sin(ac), jnp.zeros((t, DH - ROPE), F32)], axis=1)
    cos = jnp.concatenate([jnp.tile(cos, (b, 1)), jnp.ones((b * tc, DH), F32)], axis=0)
    sin = jnp.concatenate([jnp.tile(sin, (b, 1)), jnp.zeros((b * tc, DH), F32)], axis=0)
    return cos, sin


def _pick(n, *cands):
    for cand in cands:
        if n % cand == 0:
            return cand
    raise ValueError(f"no tile for {n}")


def kernel(x, c, ctx, c_ctx, w_mod, b_mod, norm_gains, w_in_ab, mla_q_norm, w_q_up, mla_kv_norm, w_kv_up, hgrn_lb, hgrn_norm, w_out_ab, conv_w_pw1, conv_b_pw1, conv_w_dw, conv_b_dw, conv_ln_g, conv_ln_b, conv_w_pw2, conv_b_pw2, ffn_w_up, ffn_w_conv, ffn_b_conv, ffn_w_down):
    b, t, _ = x.shape
    tc = ctx.shape[1]
    depth = w_mod.shape[0]
    n_lat = b * t
    n_ctx = b * tc
    assert x.shape[2] == D and depth % 2 == 0 and b < 8
    assert t % 512 == 0 and tc % 256 == 0 and n_ctx % 512 == 0 and t % n_ctx == 0
    assert t & (t - 1) == 0 and tc & (tc - 1) == 0
    tr = _pick(n_ctx, 1024, 512)
    lat_tiles = n_lat // tr
    ctx_tiles = n_ctx // tr

    hcur = jnp.concatenate([x.reshape(n_lat, D), ctx.reshape(n_ctx, D)], axis=0)
    cc = jnp.concatenate([c, c_ctx[None, :], jnp.zeros((7 - b, D), F32)], axis=0)
    mods = modulation(cc, w_mod, b_mod).reshape(depth, 8, 1, 6 * D)
    cos, sin = _rope_tables(b, t, tc)
    masks_np, lvl_np = _gla_tables()
    masks = jnp.asarray(masks_np, BF16)
    lvl = jnp.asarray(lvl_np)

    u = norm_mod(hcur, norm_gains[0, 0], mods[0], n_lat, t, shift=0, scale=1)
    for l in range(depth):
        last = l == depth - 1
        j = l // 2
        m = n_lat if last else n_lat + n_ctx
        row_tiles = m // tr
        g_pre1, g_post1, g_pre2, g_post2 = norm_gains[l]
        if l % 2 == 0:
            w_in = w_in_ab[j]
            w_in_p = jnp.concatenate([w_in[:, :QL + KVL + ROPE], jnp.zeros((D, P_FRONT - QL - KVL - ROPE), F32),
                                      w_in[:, QL + KVL + ROPE:]], axis=1).astype(BF16)
            wq = jnp.pad(w_q_up[j].reshape(QL, H, DH + ROPE), ((0, 0), (0, 0), (0, DH - ROPE)))
            wq = wq.reshape(QL, 2 * H * DH).astype(BF16)
            wkv = w_kv_up[j].reshape(KVL, H, 2 * DH)
            wkv = jnp.concatenate([wkv[:, :, :DH].reshape(KVL, H * DH), wkv[:, :, DH:].reshape(KVL, H * DH)], axis=1)
            p = matmul(u, w_in_p, _pick(m, 1024, 512), 512)
            q, k, v = mla_prep(p, mla_q_norm[j], wq, mla_kv_norm[j], wkv.astype(BF16), cos, sin)
            att_lat = attention_latent(q, k, v, b, t, tc)
            att_ctx = attention_context(q, k, v, b, t, tc)
            o_f, o_b = gla(p, hgrn_lb, masks, lvl, b, t, tc, j)
            hg = hg_merge(o_f, o_b, p, hgrn_norm[j])
            att_pieces = [(att_lat, 0, lat_tiles)] + ([] if last else [(att_ctx, lat_tiles, ctx_tiles)])
            groups = [att_pieces, [(hg, 0, row_tiles)]]
            w_o, b_o = w_out_ab, None
        else:
            glu = matmul_gated(u, conv_w_pw1, j, None, conv_b_pw1, m=m, tm=tr, tn=512, n_lat=n_lat, seq_lat=t,
                               seq_ctx=tc, silu=False, out_dtype=F32, name="conformer_glu")
            mix = dwconv_ln(glu, conv_w_dw[j], conv_b_dw[j], conv_ln_g[j], conv_ln_b[j], n_lat, t)
            groups = [[(mix, 0, row_tiles)]]
            w_o, b_o = conv_w_pw2, conv_b_pw2[j]
        hcur, u = matmul_residual(groups, w_o, j, b_o, hcur, m, g_post1, mods[l], 2, n_lat, t, tr,
                                  nxt=(g_pre2, mods[l], 3, 4))
        act = matmul_gated(u, ffn_w_up, l, ffn_w_conv, ffn_b_conv, m=m, tm=tr, tn=512, n_lat=n_lat, seq_lat=t,
                           seq_ctx=tc, silu=True, out_dtype=BF16, name="ffn_up")
        nxt = None if last else (norm_gains[l + 1, 0], mods[l + 1], 0, 1)
        hcur, u = matmul_residual([[(act, 0, row_tiles)]], ffn_w_down, l, None, hcur, m, g_post2, mods[l], 5,
                                  n_lat, t, tr, nxt=nxt)
    return hcur.reshape(b, t, D)
```

```python
import functools

import numpy as np
import jax
import jax.numpy as jnp
from jax import lax
from jax.experimental import pallas as pl
from jax.experimental.pallas import tpu as pltpu

F32 = jnp.float32
BF16 = jnp.bfloat16

D = 2048
H = D // 256
DH = 128
ROPE = 64
QL = D // 4
KVL = D // 8
DFF = (11 * D) // 4
GRID_W = 64
ROPE_BASE = 10000.0
CONV_W = 31
CONV_HALO = 16
EPS = 1e-6
ATTN_SCALE = (DH + ROPE) ** -0.5
Q_SCALE = ATTN_SCALE * float(np.log2(np.e))
ATTN_HB = 2
P_FRONT = 1024
P_WIDTH = P_FRONT + 5 * H * DH
GLA_C = 128
GLA_LEVELS = 7
GLA_HB = 4
SUBLANES = 8
VMEM_LIMIT = 56 * 1024 * 1024


def _cp(*sem):
    return pltpu.CompilerParams(dimension_semantics=sem, vmem_limit_bytes=VMEM_LIMIT)


def _sigmoid(x):
    return 1.0 / (1.0 + jnp.exp(-x))


def _rms(x):
    return x * lax.rsqrt(jnp.mean(x * x, axis=-1, keepdims=True) + EPS)


def _dot(a, b):
    return jnp.dot(a, b, preferred_element_type=F32)


def _dot_nt(a, b):
    return lax.dot_general(a, b, (((1,), (1,)), ((), ())), preferred_element_type=F32)


def _dot_tn(a, b):
    return lax.dot_general(a, b, (((0,), (0,)), ((), ())), preferred_element_type=F32)


def _mod_row(i, tm, n_lat, seq):
    return jnp.where(i < n_lat // tm, i // (seq // tm), n_lat // seq)


def _mod_body(c_ref, w_ref, b_ref, o_ref):
    c = c_ref[...]
    s = (c * _sigmoid(c)).astype(BF16)
    o_ref[0] = _dot(s, w_ref[0].astype(BF16)) + b_ref[0]


def modulation(cc, w_mod, b_mod):
    depth, _, n = w_mod.shape
    rows = cc.shape[0]
    tn = 1024
    return pl.pallas_call(
        _mod_body,
        grid=(depth, n // tn),
        in_specs=[pl.BlockSpec((rows, D), lambda l, j: (0, 0)),
                  pl.BlockSpec((1, D, tn), lambda l, j: (l, 0, j)),
                  pl.BlockSpec((1, 1, tn), lambda l, j: (l, 0, j))],
        out_specs=pl.BlockSpec((1, rows, tn), lambda l, j: (l, 0, j)),
        out_shape=jax.ShapeDtypeStruct((depth, rows, n), F32),
        compiler_params=_cp("parallel", "parallel"),
        name="modulation",
    )(cc, w_mod, b_mod.reshape(depth, 1, n))


def _norm_mod_body(x_ref, g_ref, m_ref, o_ref, *, shift, scale):
    m = m_ref[0]
    y = _rms(x_ref[...]) * g_ref[...]
    o_ref[...] = (y * (1.0 + m[:, scale * D:(scale + 1) * D]) + m[:, shift * D:(shift + 1) * D]).astype(BF16)


def norm_mod(x, gain, mods, n_lat, seq, shift, scale):
    m = x.shape[0]
    tm = 256
    return pl.pallas_call(
        functools.partial(_norm_mod_body, shift=shift, scale=scale),
        grid=(m // tm,),
        in_specs=[pl.BlockSpec((tm, D), lambda i: (i, 0)),
                  pl.BlockSpec((1, D), lambda i: (0, 0)),
                  pl.BlockSpec((1, 1, 6 * D), lambda i: (_mod_row(i, tm, n_lat, seq), 0, 0))],
        out_specs=pl.BlockSpec((tm, D), lambda i: (i, 0)),
        out_shape=jax.ShapeDtypeStruct((m, D), BF16),
        compiler_params=_cp("parallel"),
        name="norm_mod",
    )(x, gain.reshape(1, D), mods)


def _mm_body(x_ref, w_ref, o_ref):
    o_ref[...] = _dot(x_ref[...], w_ref[...])


def matmul(x, w, tm, tn):
    m, k = x.shape
    n = w.shape[1]
    return pl.pallas_call(
        _mm_body,
        grid=(m // tm, n // tn),
        in_specs=[pl.BlockSpec((tm, k), lambda i, j: (i, 0)),
                  pl.BlockSpec((k, tn), lambda i, j: (0, j))],
        out_specs=pl.BlockSpec((tm, tn), lambda i, j: (i, j)),
        out_shape=jax.ShapeDtypeStruct((m, n), F32),
        compiler_params=_cp("parallel", "parallel"),
        name="in_proj",
    )(x, w)


def _mla_prep_body(p_ref, qn_ref, wq_ref, kvn_ref, wkv_ref, cos_ref, sin_ref, q_ref, k_ref, v_ref):
    p = p_ref[...]
    tm = p.shape[0]
    cos = cos_ref[...]
    sin = sin_ref[...]
    lane = lax.broadcasted_iota(jnp.int32, (tm, DH), 1)
    first_half = (lane & 31) < 16

    def rope(x):
        partner = jnp.where(first_half, pltpu.roll(x, DH - 16, 1), pltpu.roll(x, 16, 1))
        return x * cos + partner * sin

    cq = (_rms(p[:, :QL]) * qn_ref[...]).astype(BF16)
    ckv = (_rms(p[:, QL:QL + KVL]) * kvn_ref[...]).astype(BF16)
    kr = rope(p[:, QL + KVL:QL + KVL + DH]).astype(BF16)
    q = _dot(cq, wq_ref[...])
    kv = _dot(ckv, wkv_ref[...])
    for h in range(H):
        lo = 2 * DH * h
        q_ref[:, lo:lo + DH] = (q[:, lo:lo + DH] * Q_SCALE).astype(BF16)
        q_ref[:, lo + DH:lo + 2 * DH] = (rope(q[:, lo + DH:lo + 2 * DH]) * Q_SCALE).astype(BF16)
        k_ref[:, lo:lo + DH] = kv[:, DH * h:DH * (h + 1)].astype(BF16)
        k_ref[:, lo + DH:lo + 2 * DH] = kr
    v_ref[...] = kv[:, H * DH:].astype(BF16)


def mla_prep(p, q_norm, wq, kv_norm, wkv, cos, sin):
    m = p.shape[0]
    tm = 512
    row = lambda i: (i, 0)
    fixed = lambda i: (0, 0)
    return pl.pallas_call(
        _mla_prep_body,
        grid=(m // tm,),
        in_specs=[pl.BlockSpec((tm, P_FRONT), row),
                  pl.BlockSpec((1, QL), fixed),
                  pl.BlockSpec((QL, 2 * H * DH), fixed),
                  pl.BlockSpec((1, KVL), fixed),
                  pl.BlockSpec((KVL, 2 * H * DH), fixed),
                  pl.BlockSpec((tm, DH), row),
                  pl.BlockSpec((tm, DH), row)],
        out_specs=[pl.BlockSpec((tm, 2 * H * DH), row),
                   pl.BlockSpec((tm, 2 * H * DH), row),
                   pl.BlockSpec((tm, H * DH), row)],
        out_shape=[jax.ShapeDtypeStruct((m, 2 * H * DH), BF16),
                   jax.ShapeDtypeStruct((m, 2 * H * DH), BF16),
                   jax.ShapeDtypeStruct((m, H * DH), BF16)],
        compiler_params=_cp("parallel"),
        name="mla_prep",
    )(p, q_norm.reshape(1, QL), wq, kv_norm.reshape(1, KVL), wkv, cos, sin)


def _attn_lat_body(q_ref, kl_ref, vl_ref, kc_ref, vc_ref, o_ref):
    for hh in range(ATTN_HB):
        qk = slice(hh * 2 * DH, (hh + 1) * 2 * DH)
        vo = slice(hh * DH, (hh + 1) * DH)
        q = q_ref[:, qk]
        sl = _dot_nt(q, kl_ref[:, qk])
        sc = _dot_nt(q, kc_ref[:, qk])
        mx = jnp.maximum(jnp.max(sl, axis=-1, keepdims=True), jnp.max(sc, axis=-1, keepdims=True))
        el = jnp.exp2(sl - mx)
        ec = jnp.exp2(sc - mx)
        den = jnp.sum(el, axis=-1, keepdims=True) + jnp.sum(ec, axis=-1, keepdims=True)
        o = _dot(el.astype(BF16), vl_ref[:, vo]) + _dot(ec.astype(BF16), vc_ref[:, vo])
        o_ref[:, vo] = (o / den).astype(BF16)


def _attn_ctx_body(q_ref, kc_ref, vc_ref, o_ref):
    sc = _dot_nt(q_ref[...], kc_ref[...])
    ec = jnp.exp2(sc - jnp.max(sc, axis=-1, keepdims=True))
    o = _dot(ec.astype(BF16), vc_ref[...])
    o_ref[...] = (o / jnp.sum(ec, axis=-1, keepdims=True)).astype(BF16)


def attention_latent(q, k, v, b, t, tc):
    n_lat = b * t
    tq = 512
    nq = t // tq
    cb = n_lat // tc
    wqk = ATTN_HB * 2 * DH
    wv = ATTN_HB * DH
    return pl.pallas_call(
        _attn_lat_body,
        grid=(b, H // ATTN_HB, nq),
        in_specs=[pl.BlockSpec((tq, wqk), lambda bi, h, qi: (bi * nq + qi, h)),
                  pl.BlockSpec((t, wqk), lambda bi, h, qi: (bi, h)),
                  pl.BlockSpec((t, wv), lambda bi, h, qi: (bi, h)),
                  pl.BlockSpec((tc, wqk), lambda bi, h, qi: (cb + bi, h)),
                  pl.BlockSpec((tc, wv), lambda bi, h, qi: (cb + bi, h))],
        out_specs=pl.BlockSpec((tq, wv), lambda bi, h, qi: (bi * nq + qi, h)),
        out_shape=jax.ShapeDtypeStruct((n_lat, H * DH), BF16),
        compiler_params=_cp("parallel", "parallel", "parallel"),
        name="attn_latent",
    )(q, k, v, k, v)


def attention_context(q, k, v, b, t, tc):
    cb = (b * t) // tc
    return pl.pallas_call(
        _attn_ctx_body,
        grid=(b, H),
        in_specs=[pl.BlockSpec((tc, 2 * DH), lambda bi, h: (cb + bi, h)),
                  pl.BlockSpec((tc, 2 * DH), lambda bi, h: (cb + bi, h)),
                  pl.BlockSpec((tc, DH), lambda bi, h: (cb + bi, h))],
        out_specs=pl.BlockSpec((tc, DH), lambda bi, h: (bi, h)),
        out_shape=jax.ShapeDtypeStruct((b * tc, H * DH), BF16),
        compiler_params=_cp("parallel", "parallel"),
        name="attn_context",
    )(q, k, v)


def _gla_tables():
    c = GLA_C
    t = np.arange(c)[:, None]
    r = np.arange(c)[None, :]
    mats = []
    for lv in range(GLA_LEVELS):
        same = (t >> (lv + 1)) == (r >> (lv + 1))
        bt = (t >> lv) & 1
        br = (r >> lv) & 1
        upper = same & (bt == 1) & (br == 1) & (r <= t)
        lower = same & (bt == 0) & (br == 0) & (r > t)
        mats.append(upper | lower)
    mats.append(r <= t)
    mats.append(r > t)
    fwd = np.stack(mats).astype(np.float32)
    bwd = fwd[:, ::-1, ::-1]
    masks = np.stack([fwd, bwd]).reshape(2, (GLA_LEVELS + 2) * c, c)
    masks = np.concatenate([masks, masks], axis=2)
    x = t ^ r
    lvl = np.where(x == 0, -1, np.floor(np.log2(np.maximum(x, 1)))).astype(np.int32)
    return masks, lvl


def _gla_gates(x, lb_raw, layer):
    log_sig = jnp.minimum(x, 0.0) - jnp.log1p(jnp.exp(-jnp.abs(x)))
    if layer == 0:
        g = log_sig
        kk = 1.0 / (1.0 + jnp.exp(x))
    else:
        e = jnp.exp(lb_raw - jnp.max(lb_raw, axis=0, keepdims=True))
        prob = e / jnp.sum(e, axis=0, keepdims=True)
        lb = jnp.sum(prob[1:layer + 1], axis=0, keepdims=True)
        a = jnp.log(lb)
        bb = jnp.log1p(-lb) + log_sig
        g = jnp.maximum(a, bb) + jnp.log1p(jnp.exp(-jnp.abs(a - bb)))
        kk = (1.0 - lb) / (1.0 + jnp.exp(x))
    return g, kk


def _gla_chain(q, kk, v, ex, decay, level_is, row_bit, st_ref, slot, *, d):
    c = GLA_C
    v = v.astype(BF16)
    att = jnp.where(level_is[-1], _dot_nt(q.astype(BF16), kk.astype(BF16)), 0.0)
    for lv in range(GLA_LEVELS):
        ex_l = ex[lv * c:(lv + 1) * c]
        half = 1 << lv
        if half < SUBLANES:
            is_q = row_bit[lv] if d == 0 else jnp.logical_not(row_bit[lv])
            scaled = jnp.where(is_q, q, kk) * ex_l
            ql = jnp.where(is_q, scaled, 0.0)
            kl = jnp.where(is_q, 0.0, scaled)
        else:
            zeros = jnp.zeros((half, DH), F32)
            qparts, kparts = [], []
            for blk in range(c // half):
                rows = slice(blk * half, (blk + 1) * half)
                if (blk & 1) == 1 - d:
                    qparts.append(q[rows] * ex_l[rows])
                    kparts.append(zeros)
                else:
                    qparts.append(zeros)
                    kparts.append(kk[rows] * ex_l[rows])
            ql = jnp.concatenate(qparts, axis=0)
            kl = jnp.concatenate(kparts, axis=0)
        att = jnp.where(level_is[lv], _dot_nt(ql.astype(BF16), kl.astype(BF16)), att)

    q_in = (q * ex[GLA_LEVELS * c:(GLA_LEVELS + 1) * c]).astype(BF16)
    k_out = (kk * ex[(GLA_LEVELS + 1) * c:]).astype(BF16)
    st = st_ref[slot]
    o = _dot(att.astype(BF16), v) + _dot_nt(q_in, st.astype(BF16))
    st_ref[slot] = st * decay + _dot_tn(v, k_out)
    return o


def _gla_body(qf_ref, ff_ref, vf_ref, qb_ref, fb_ref, vb_ref, lb_ref, msk_ref, lvl_ref, of_ref, ob_ref, st_ref,
              *, layer):
    c = GLA_C

    @pl.when(pl.program_id(2) == 0)
    def _():
        st_ref[...] = jnp.zeros_like(st_ref)

    lvl = lvl_ref[...]
    level_is = [lvl == lv for lv in range(GLA_LEVELS)] + [lvl == -1]
    row = lax.broadcasted_iota(jnp.int32, (c, DH), 0)
    row_bit = [((row >> lv) & 1) == 1 for lv in range(GLA_LEVELS)]
    for d, (q_ref, f_ref, v_ref, o_ref) in enumerate(((qf_ref, ff_ref, vf_ref, of_ref),
                                                      (qb_ref, fb_ref, vb_ref, ob_ref))):
        g, kk = _gla_gates(f_ref[...], lb_ref[d], layer)
        g_hi = g.astype(BF16)
        g_lo = (g - g_hi.astype(F32)).astype(BF16)
        ex = jnp.exp(_dot(msk_ref[d], jnp.concatenate([g_hi, g_lo], axis=0)))
        decay = jnp.exp(jnp.sum(g, axis=0, keepdims=True))
        for hh in range(GLA_HB):
            cols = slice(hh * DH, (hh + 1) * DH)
            o_ref[:, cols] = _gla_chain(q_ref[:, cols], kk[:, cols], v_ref[:, cols], ex[:, cols], decay[:, cols],
                                        level_is, row_bit, st_ref, d * GLA_HB + hh, d=d)


def gla(p, lb_raw, masks, lvl, b, t, tc, layer):
    c = GLA_C
    m = p.shape[0]
    n_ctx_chunks = tc // c
    n_lat_chunks = t // c
    n_chunks = n_ctx_chunks + n_lat_chunks
    ctx0 = (b * t) // c
    n_even = lb_raw.shape[1]
    wb = GLA_HB * DH

    def blk(bi, i, d):
        jc = i if d == 0 else n_ctx_chunks - 1 - i
        jl = i - n_ctx_chunks if d == 0 else n_chunks - 1 - i
        return jnp.where(i < n_ctx_chunks, ctx0 + bi * n_ctx_chunks + jc, bi * n_lat_chunks + jl)

    def spec(section, d):
        col0 = (P_FRONT + section * H * DH) // wb
        return pl.BlockSpec((c, wb), lambda bi, hh, i: (blk(bi, i, d), col0 + hh))

    out_spec = lambda d: pl.BlockSpec((c, wb), lambda bi, hh, i: (blk(bi, i, d), hh))
    return pl.pallas_call(
        functools.partial(_gla_body, layer=layer),
        grid=(b, H // GLA_HB, n_chunks),
        in_specs=[spec(0, 0), spec(1, 0), spec(3, 0), spec(0, 1), spec(2, 1), spec(3, 1),
                  pl.BlockSpec((2, n_even, wb), lambda bi, hh, i: (0, 0, hh)),
                  pl.BlockSpec((2, (GLA_LEVELS + 2) * c, 2 * c), lambda bi, hh, i: (0, 0, 0)),
                  pl.BlockSpec((c, c), lambda bi, hh, i: (0, 0))],
        out_specs=[out_spec(0), out_spec(1)],
        out_shape=[jax.ShapeDtypeStruct((m, H * DH), F32)] * 2,
        scratch_shapes=[pltpu.VMEM((2 * GLA_HB, DH, DH), F32)],
        compiler_params=_cp("parallel", "parallel", "arbitrary"),
        name="hgrn_scan",
    )(p, p, p, p, p, p, lb_raw, masks, lvl)


def _hg_merge_body(of_ref, ob_ref, g_ref, n_ref, y_ref):
    o = of_ref[...] + ob_ref[...]
    gate = g_ref[...]
    gain = n_ref[...]
    for h in range(H):
        sl = slice(h * DH, (h + 1) * DH)
        gh = gate[:, sl]
        y_ref[:, sl] = (_rms(o[:, sl]) * gain * (gh * _sigmoid(gh))).astype(BF16)


def hg_merge(o_f, o_b, p, hg_norm):
    m = p.shape[0]
    tm = 512
    w = H * DH
    row = lambda i: (i, 0)
    return pl.pallas_call(
        _hg_merge_body,
        grid=(m // tm,),
        in_specs=[pl.BlockSpec((tm, w), row),
                  pl.BlockSpec((tm, w), row),
                  pl.BlockSpec((tm, w), lambda i: (i, (P_WIDTH - w) // w)),
                  pl.BlockSpec((1, DH), lambda i: (0, 0))],
        out_specs=pl.BlockSpec((tm, w), row),
        out_shape=jax.ShapeDtypeStruct((m, w), BF16),
        compiler_params=_cp("parallel"),
        name="hgrn_merge",
    )(o_f, o_b, p, hg_norm.reshape(1, DH))


RES_TN = 256
RES_ROWS = 128
SINGLE_BUFFER_BYTES = 4 * 1024 * 1024


def _mm_res_body(*refs, groups, gate, shift, scale, nj, has_bias, has_next):
    refs = list(refs)
    a_refs = [[refs.pop(0) for _ in pieces] for _, pieces in groups]
    w_ref = refs.pop(0)
    b_ref = refs.pop(0) if has_bias else None
    x_ref, gp_ref, mc_ref = refs.pop(0), refs.pop(0), refs.pop(0)
    gn_ref, mn_ref = (refs.pop(0), refs.pop(0)) if has_next else (None, None)
    xo_ref = refs.pop(0)
    uo_ref = refs.pop(0) if has_next else None
    i = pl.program_id(0)
    j = pl.program_id(1)
    tn = w_ref.shape[1]
    cols = pl.ds(pl.multiple_of(j * tn, tn), tn)

    w = w_ref[...].astype(BF16)
    koff = 0
    for gi, ((kw, pieces), piece_refs) in enumerate(zip(groups, a_refs)):
        wk = w[koff:koff + kw]
        koff += kw
        for a_ref, (r0, nr) in zip(piece_refs, pieces):
            def project(a_ref=a_ref, wk=wk, gi=gi):
                part = _dot(a_ref[...], wk)
                if gi == 0:
                    xo_ref[:, cols] = part
                else:
                    xo_ref[:, cols] += part

            if len(pieces) == 1:
                project()
            else:
                pl.when((i >= r0) & (i < r0 + nr))(project)

    @pl.when(j == nj - 1)
    def _():
        mc = mc_ref[0]
        gate_v = mc[:, gate * D:(gate + 1) * D]
        gp = gp_ref[...]
        if has_next:
            mn = mn_ref[0]
            gn = gn_ref[...]
            scale_v = 1.0 + mn[:, scale * D:(scale + 1) * D]
            shift_v = mn[:, shift * D:(shift + 1) * D]

        def rows_pass(r, carry):
            rows = pl.ds(pl.multiple_of(r * RES_ROWS, RES_ROWS), RES_ROWS)
            y = xo_ref[rows, :]
            if has_bias:
                y = y + b_ref[...]
            xn = x_ref[rows, :] + gate_v * (_rms(y) * gp)
            xo_ref[rows, :] = xn
            if has_next:
                uo_ref[rows, :] = (_rms(xn) * gn * scale_v + shift_v).astype(BF16)
            return carry

        lax.fori_loop(0, xo_ref.shape[0] // RES_ROWS, rows_pass, 0)


def matmul_residual(groups, w, layer, bias, x, m, g_post, mods_cur, gate, n_lat, seq, tm, nxt=None):
    tn = RES_TN
    nj = D // tn
    kdim = w.shape[1]
    mrow = lambda i, j: (_mod_row(i, tm, n_lat, seq), 0, 0)
    row = lambda i, j: (i, 0)
    fixed = lambda i, j: (0, 0)
    once = pl.Buffered(1)

    def piece_spec(arr, r0, nr):
        mode = {"pipeline_mode": once} if tm * arr.shape[1] * arr.dtype.itemsize >= SINGLE_BUFFER_BYTES else {}
        return pl.BlockSpec((tm, arr.shape[1]), lambda i, j: (jnp.clip(i - r0, 0, nr - 1), 0), **mode)

    in_specs = [piece_spec(*pc) for g in groups for pc in g]
    in_specs.append(pl.BlockSpec((None, kdim, tn), lambda i, j: (layer, 0, j)))
    args = [pc[0] for g in groups for pc in g] + [w]
    if bias is not None:
        in_specs.append(pl.BlockSpec((1, D), fixed))
        args.append(bias.reshape(1, D))
    in_specs += [pl.BlockSpec((tm, D), row, pipeline_mode=once),
                 pl.BlockSpec((1, D), fixed),
                 pl.BlockSpec((1, 1, 6 * D), mrow)]
    args += [x, g_post.reshape(1, D), mods_cur]
    out_specs = [pl.BlockSpec((tm, D), row, pipeline_mode=once)]
    out_shape = [jax.ShapeDtypeStruct((m, D), F32)]
    shift = scale = 0
    if nxt is not None:
        g_next, mods_next, shift, scale = nxt
        in_specs += [pl.BlockSpec((1, D), fixed), pl.BlockSpec((1, 1, 6 * D), mrow)]
        args += [g_next.reshape(1, D), mods_next]
        out_specs.append(pl.BlockSpec((tm, D), row, pipeline_mode=once))
        out_shape.append(jax.ShapeDtypeStruct((m, D), BF16))
    group_meta = tuple((g[0][0].shape[1], tuple(pc[1:] for pc in g)) for g in groups)
    assert sum(kw for kw, _ in group_meta) == kdim
    outs = pl.pallas_call(
        functools.partial(_mm_res_body, groups=group_meta, gate=gate, shift=shift, scale=scale,
                          nj=nj, has_bias=bias is not None, has_next=nxt is not None),
        grid=(m // tm, nj),
        in_specs=in_specs,
        out_specs=out_specs,
        out_shape=out_shape,
        compiler_params=_cp("parallel", "arbitrary"),
        name="proj_residual",
    )(*args)
    return (outs[0], outs[1]) if nxt is not None else (outs[0], None)


GATED_HALO = 16


def _mm_gated_body(*refs, conv, silu, lat_tiles, seq_lat, seq_ctx):
    if conv:
        x_ref, xp_ref, xn_ref, wa_ref, wg_ref, ca_ref, cg_ref, ba_ref, bg_ref, o_ref = refs
    else:
        x_ref, wa_ref, wg_ref, ba_ref, bg_ref, o_ref = refs
    x = x_ref[...]
    wa = wa_ref[...].astype(BF16)
    wg = wg_ref[...].astype(BF16)
    za = _dot(x, wa)
    zg = _dot(x, wg)
    if conv:
        i = pl.program_id(0)
        tm, tn = za.shape
        halo = jnp.concatenate([xp_ref[...], xn_ref[...]], axis=0)
        ha = _dot(halo, wa)
        hg = _dot(halo, wg)
        is_ctx = i >= lat_tiles
        seq_m1 = jnp.where(is_ctx, seq_ctx - 1, seq_lat - 1)
        starts_seq = ((i * tm) & seq_m1) == 0
        ends_seq = (((i + 1) * tm) & seq_m1) == 0
        sub = lax.broadcasted_iota(jnp.int32, (SUBLANES, tn), 0)
        top = sub == 0
        bottom = sub == SUBLANES - 1
        inner = range(seq_ctx, tm, seq_ctx)
        g = SUBLANES

        def conv3(z, h, cw):
            up = pltpu.roll(z, 1, 0)
            dn = pltpu.roll(z, tm - 1, 0)
            before = jnp.where(starts_seq, 0.0, h[GATED_HALO - 1:GATED_HALO])
            after = jnp.where(ends_seq, 0.0, h[GATED_HALO:GATED_HALO + 1])
            up_parts, dn_parts = [jnp.where(top, before, up[:g])], []
            up_lo, dn_lo = g, 0
            for r in inner:
                up_parts += [up[up_lo:r], jnp.where(jnp.logical_and(top, is_ctx), 0.0, up[r:r + g])]
                dn_parts += [dn[dn_lo:r - g], jnp.where(jnp.logical_and(bottom, is_ctx), 0.0, dn[r - g:r])]
                up_lo, dn_lo = r + g, r
            up_parts.append(up[up_lo:])
            dn_parts += [dn[dn_lo:tm - g], jnp.where(bottom, after, dn[tm - g:])]
            up = jnp.concatenate(up_parts, axis=0)
            dn = jnp.concatenate(dn_parts, axis=0)
            return cw[0:1] * up + cw[1:2] * z + cw[2:3] * dn

        za = conv3(za, ha, ca_ref[...])
        zg = conv3(zg, hg, cg_ref[...])
    za = za + ba_ref[...]
    zg = zg + bg_ref[...]
    sg = _sigmoid(zg)
    o_ref[...] = (za * (zg * sg if silu else sg)).astype(o_ref.dtype)


def matmul_gated(x, w, layer, conv_w, bias, *, m, tm, tn, n_lat, seq_lat, seq_ctx, silu, out_dtype, name):
    kdim = x.shape[1]
    n = w.shape[2] // 2
    nj = n // tn
    first = lambda i, j: (layer, 0, j)
    second = lambda i, j: (layer, 0, j + nj)
    bias = bias.reshape(bias.shape[0], 1, 2 * n)
    in_specs = [pl.BlockSpec((tm, kdim), lambda i, j: (i, 0))]
    args = [x]
    if conv_w is not None:
        hb = tm // GATED_HALO
        last_hb = x.shape[0] // GATED_HALO - 1
        in_specs += [pl.BlockSpec((GATED_HALO, kdim), lambda i, j: (jnp.maximum(i * hb - 1, 0), 0)),
                     pl.BlockSpec((GATED_HALO, kdim), lambda i, j: (jnp.minimum((i + 1) * hb, last_hb), 0))]
        args += [x, x]
    in_specs += [pl.BlockSpec((None, kdim, tn), first), pl.BlockSpec((None, kdim, tn), second)]
    args += [w, w]
    if conv_w is not None:
        in_specs += [pl.BlockSpec((None, conv_w.shape[1], tn), first),
                     pl.BlockSpec((None, conv_w.shape[1], tn), second)]
        args += [conv_w, conv_w]
    in_specs += [pl.BlockSpec((None, 1, tn), first), pl.BlockSpec((None, 1, tn), second)]
    args += [bias, bias]
    return pl.pallas_call(
        functools.partial(_mm_gated_body, conv=conv_w is not None, silu=silu, lat_tiles=n_lat // tm,
                          seq_lat=seq_lat, seq_ctx=seq_ctx),
        grid=(m // tm, nj),
        in_specs=in_specs,
        out_specs=pl.BlockSpec((tm, tn), lambda i, j: (i, j)),
        out_shape=jax.ShapeDtypeStruct((m, n), out_dtype),
        compiler_params=_cp("parallel", "parallel"),
        name=name,
    )(*args)


def _dwconv_ln_body(prev_ref, cur_ref, next_ref, w_ref, b_ref, g_ref, beta_ref, o_ref, ext_ref, sh_ref, acc_ref,
                    *, tiles_per_seq, n_lat_tiles):
    i = pl.program_id(0)
    tm = cur_ref.shape[0]
    hw = CONV_HALO
    is_lat = i < n_lat_tiles
    j = i % tiles_per_seq
    has_prev = jnp.logical_and(is_lat, j > 0).astype(F32)
    has_next = jnp.logical_and(is_lat, j < tiles_per_seq - 1).astype(F32)
    ext_ref[0:hw, :] = prev_ref[tm - hw:, :] * has_prev
    ext_ref[hw:hw + tm, :] = cur_ref[...]
    ext_ref[hw + tm:, :] = next_ref[0:hw, :] * has_next

    sh_rows = sh_ref.shape[1]

    def lane_block(cb, carry):
        col = pl.multiple_of(cb * DH, DH)
        wv = w_ref[:, pl.ds(col, DH)]
        for s in range(SUBLANES):
            sh_ref[s] = ext_ref[s:s + sh_rows, pl.ds(col, DH)]
        for r0 in range(0, tm, 128):
            acc = jnp.zeros((128, DH), F32)
            for k in range(CONV_W):
                off = r0 + k + hw - (CONV_W - 1) // 2
                s = off % SUBLANES
                acc = acc + sh_ref[s, off - s:off - s + 128, :] * wv[k:k + 1]
            acc_ref[r0:r0 + 128, pl.ds(col, DH)] = acc
        return carry

    lax.fori_loop(0, D // DH, lane_block, 0)
    hcv = acc_ref[...] + b_ref[...]
    mu = jnp.mean(hcv, axis=-1, keepdims=True)
    xc = hcv - mu
    var = jnp.mean(xc * xc, axis=-1, keepdims=True)
    y = xc * lax.rsqrt(var + EPS) * g_ref[...] + beta_ref[...]
    o_ref[...] = (y * _sigmoid(y)).astype(BF16)


def dwconv_ln(hx, w_dw, b_dw, ln_g, ln_b, n_lat, seq):
    m = hx.shape[0]
    tm = 256
    nt = m // tm
    fixed = lambda i: (0, 0)
    vec = lambda a: a.reshape(1, D)
    return pl.pallas_call(
        functools.partial(_dwconv_ln_body, tiles_per_seq=seq // tm, n_lat_tiles=n_lat // tm),
        grid=(nt,),
        in_specs=[pl.BlockSpec((tm, D), lambda i: (jnp.maximum(i - 1, 0), 0)),
                  pl.BlockSpec((tm, D), lambda i: (i, 0)),
                  pl.BlockSpec((tm, D), lambda i: (jnp.minimum(i + 1, nt - 1), 0)),
                  pl.BlockSpec((CONV_W, D), fixed),
                  pl.BlockSpec((1, D), fixed),
                  pl.BlockSpec((1, D), fixed),
                  pl.BlockSpec((1, D), fixed)],
        out_specs=pl.BlockSpec((tm, D), lambda i: (i, 0)),
        out_shape=jax.ShapeDtypeStruct((m, D), BF16),
        scratch_shapes=[pltpu.VMEM((tm + 2 * CONV_HALO, D), F32),
                        pltpu.VMEM((SUBLANES, tm + 2 * CONV_HALO - SUBLANES, DH), F32),
                        pltpu.VMEM((tm, D), F32)],
        compiler_params=_cp("parallel"),
        name="dwconv_ln",
    )(hx, hx, hx, w_dw, vec(b_dw), vec(ln_g), vec(ln_b))


def _rope_tables(b, t, tc):
    rows = t // GRID_W
    row = jnp.repeat(jnp.arange(rows), GRID_W).astype(F32)
    col = jnp.tile(jnp.arange(GRID_W), rows).astype(F32)
    pairs = ROPE // 4
    freqs = ROPE_BASE ** (-jnp.arange(pairs, dtype=F32) / pairs)
    ar = row[:, None] * freqs
    ac = col[:, None] * freqs
    cos = jnp.concatenate([jnp.cos(ar), jnp.cos(ar), jnp.cos(ac), jnp.cos(ac), jnp.ones((t, DH - ROPE), F32)], axis=1)
    sin = jnp.concatenate([-jnp.sin(ar), jnp.sin(ar), -jnp.sin(ac), jnp.sin(ac), jnp.zeros((t, DH - ROPE), F32)], axis=1)
    cos = jnp.concatenate([jnp.tile(cos, (b, 1)), jnp.ones((b * tc, DH), F32)], axis=0)
    sin = jnp.concatenate([jnp.tile(sin, (b, 1)), jnp.zeros((b * tc, DH), F32)], axis=0)
    return cos, sin


def _pick(n, *cands):
    for cand in cands:
        if n % cand == 0:
            return cand
    raise ValueError(f"no tile for {n}")


def kernel(x, c, ctx, c_ctx, w_mod, b_mod, norm_gains, w_in_ab, mla_q_norm, w_q_up, mla_kv_norm, w_kv_up, hgrn_lb, hgrn_norm, w_out_ab, conv_w_pw1, conv_b_pw1, conv_w_dw, conv_b_dw, conv_ln_g, conv_ln_b, conv_w_pw2, conv_b_pw2, ffn_w_up, ffn_w_conv, ffn_b_conv, ffn_w_down):
    b, t, _ = x.shape
    tc = ctx.shape[1]
    depth = w_mod.shape[0]
    n_lat = b * t
    n_ctx = b * tc
    assert x.shape[2] == D and depth % 2 == 0 and b < 8
    assert t % 512 == 0 and tc % 256 == 0 and n_ctx % 512 == 0 and t % n_ctx == 0
    assert t & (t - 1) == 0 and tc & (tc - 1) == 0
    tr = _pick(n_ctx, 1024, 512)
    lat_tiles = n_lat // tr
    ctx_tiles = n_ctx // tr

    hcur = jnp.concatenate([x.reshape(n_lat, D), ctx.reshape(n_ctx, D)], axis=0)
    cc = jnp.concatenate([c, c_ctx[None, :], jnp.zeros((7 - b, D), F32)], axis=0)
    mods = modulation(cc, w_mod, b_mod).reshape(depth, 8, 1, 6 * D)
    cos, sin = _rope_tables(b, t, tc)
    masks_np, lvl_np = _gla_tables()
    masks = jnp.asarray(masks_np, BF16)
    lvl = jnp.asarray(lvl_np)

    u = norm_mod(hcur, norm_gains[0, 0], mods[0], n_lat, t, shift=0, scale=1)
    for l in range(depth):
        last = l == depth - 1
        j = l // 2
        m = n_lat if last else n_lat + n_ctx
        row_tiles = m // tr
        g_pre1, g_post1, g_pre2, g_post2 = norm_gains[l]
        if l % 2 == 0:
            w_in = w_in_ab[j]
            w_in_p = jnp.concatenate([w_in[:, :QL + KVL + ROPE], jnp.zeros((D, P_FRONT - QL - KVL - ROPE), F32),
                                      w_in[:, QL + KVL + ROPE:]], axis=1).astype(BF16)
            wq = jnp.pad(w_q_up[j].reshape(QL, H, DH + ROPE), ((0, 0), (0, 0), (0, DH - ROPE)))
            wq = wq.reshape(QL, 2 * H * DH).astype(BF16)
            wkv = w_kv_up[j].reshape(KVL, H, 2 * DH)
            wkv = jnp.concatenate([wkv[:, :, :DH].reshape(KVL, H * DH), wkv[:, :, DH:].reshape(KVL, H * DH)], axis=1)
            p = matmul(u, w_in_p, _pick(m, 1024, 512), 512)
            q, k, v = mla_prep(p, mla_q_norm[j], wq, mla_kv_norm[j], wkv.astype(BF16), cos, sin)
            att_lat = attention_latent(q, k, v, b, t, tc)
            att_ctx = attention_context(q, k, v, b, t, tc)
            o_f, o_b = gla(p, hgrn_lb, masks, lvl, b, t, tc, j)
            hg = hg_merge(o_f, o_b, p, hgrn_norm[j])
            att_pieces = [(att_lat, 0, lat_tiles)] + ([] if last else [(att_ctx, lat_tiles, ctx_tiles)])
            groups = [att_pieces, [(hg, 0, row_tiles)]]
            w_o, b_o = w_out_ab, None
        else:
            glu = matmul_gated(u, conv_w_pw1, j, None, conv_b_pw1, m=m, tm=tr, tn=512, n_lat=n_lat, seq_lat=t,
                               seq_ctx=tc, silu=False, out_dtype=F32, name="conformer_glu")
            mix = dwconv_ln(glu, conv_w_dw[j], conv_b_dw[j], conv_ln_g[j], conv_ln_b[j], n_lat, t)
            groups = [[(mix, 0, row_tiles)]]
            w_o, b_o = conv_w_pw2, conv_b_pw2[j]
        hcur, u = matmul_residual(groups, w_o, j, b_o, hcur, m, g_post1, mods[l], 2, n_lat, t, tr,
                                  nxt=(g_pre2, mods[l], 3, 4))
        act = matmul_gated(u, ffn_w_up, l, ffn_w_conv, ffn_b_conv, m=m, tm=tr, tn=512, n_lat=n_lat, seq_lat=t,
                           seq_ctx=tc, silu=True, out_dtype=BF16, name="ffn_up")
        nxt = None if last else (norm_gains[l + 1, 0], mods[l + 1], 0, 1)
        hcur, u = matmul_residual([[(act, 0, row_tiles)]], ffn_w_down, l, None, hcur, m, g_post2, mods[l], 5,
                                  n_lat, t, tr, nxt=nxt)
    return hcur.reshape(b, t, D)
```

```python
import functools

import numpy as np
import jax
import jax.numpy as jnp
from jax import lax
from jax.experimental import pallas as pl
from jax.experimental.pallas import tpu as pltpu

F32 = jnp.float32
BF16 = jnp.bfloat16

D = 2048
H = D // 256
DH = 128
ROPE = 64
QL = D // 4
KVL = D // 8
DFF = (11 * D) // 4
GRID_W = 64
ROPE_BASE = 10000.0
CONV_W = 31
CONV_HALO = 16
EPS = 1e-6
ATTN_SCALE = (DH + ROPE) ** -0.5
Q_SCALE = ATTN_SCALE * float(np.log2(np.e))
ATTN_HB = 2
P_FRONT = 1024
P_WIDTH = P_FRONT + 5 * H * DH
GLA_C = 128
GLA_LEVELS = 7
GLA_HB = 4
SUBLANES = 8
VMEM_LIMIT = 56 * 1024 * 1024


def _cp(*sem):
    return pltpu.CompilerParams(dimension_semantics=sem, vmem_limit_bytes=VMEM_LIMIT)


def _sigmoid(x):
    return 1.0 / (1.0 + jnp.exp(-x))


def _rms(x):
    return x * lax.rsqrt(jnp.mean(x * x, axis=-1, keepdims=True) + EPS)


def _dot(a, b):
    return jnp.dot(a, b, preferred_element_type=F32)


def _dot_nt(a, b):
    return lax.dot_general(a, b, (((1,), (1,)), ((), ())), preferred_element_type=F32)


def _dot_tn(a, b):
    return lax.dot_general(a, b, (((0,), (0,)), ((), ())), preferred_element_type=F32)


def _mod_row(i, tm, n_lat, seq):
    return jnp.where(i < n_lat // tm, i // (seq // tm), n_lat // seq)


def _mod_body(c_ref, w_ref, b_ref, o_ref):
    c = c_ref[...]
    s = (c * _sigmoid(c)).astype(BF16)
    o_ref[0] = _dot(s, w_ref[0].astype(BF16)) + b_ref[0]


def modulation(cc, w_mod, b_mod):
    depth, _, n = w_mod.shape
    rows = cc.shape[0]
    tn = 1024
    return pl.pallas_call(
        _mod_body,
        grid=(depth, n // tn),
        in_specs=[pl.BlockSpec((rows, D), lambda l, j: (0, 0)),
                  pl.BlockSpec((1, D, tn), lambda l, j: (l, 0, j)),
                  pl.BlockSpec((1, 1, tn), lambda l, j: (l, 0, j))],
        out_specs=pl.BlockSpec((1, rows, tn), lambda l, j: (l, 0, j)),
        out_shape=jax.ShapeDtypeStruct((depth, rows, n), F32),
        compiler_params=_cp("parallel", "parallel"),
        name="modulation",
    )(cc, w_mod, b_mod.reshape(depth, 1, n))


def _norm_mod_body(x_ref, g_ref, m_ref, o_ref, *, shift, scale):
    m = m_ref[0]
    y = _rms(x_ref[...]) * g_ref[...]
    o_ref[...] = (y * (1.0 + m[:, scale * D:(scale + 1) * D]) + m[:, shift * D:(shift + 1) * D]).astype(BF16)


def norm_mod(x, gain, mods, n_lat, seq, shift, scale):
    m = x.shape[0]
    tm = 256
    return pl.pallas_call(
        functools.partial(_norm_mod_body, shift=shift, scale=scale),
        grid=(m // tm,),
        in_specs=[pl.BlockSpec((tm, D), lambda i: (i, 0)),
                  pl.BlockSpec((1, D), lambda i: (0, 0)),
                  pl.BlockSpec((1, 1, 6 * D), lambda i: (_mod_row(i, tm, n_lat, seq), 0, 0))],
        out_specs=pl.BlockSpec((tm, D), lambda i: (i, 0)),
        out_shape=jax.ShapeDtypeStruct((m, D), BF16),
        compiler_params=_cp("parallel"),
        name="norm_mod",
    )(x, gain.reshape(1, D), mods)


def _mm_body(x_ref, w_ref, o_ref):
    o_ref[...] = _dot(x_ref[...], w_ref[...])


def matmul(x, w, tm, tn):
    m, k = x.shape
    n = w.shape[1]
    return pl.pallas_call(
        _mm_body,
        grid=(m // tm, n // tn),
        in_specs=[pl.BlockSpec((tm, k), lambda i, j: (i, 0)),
                  pl.BlockSpec((k, tn), lambda i, j: (0, j))],
        out_specs=pl.BlockSpec((tm, tn), lambda i, j: (i, j)),
        out_shape=jax.ShapeDtypeStruct((m, n), F32),
        compiler_params=_cp("parallel", "parallel"),
        name="in_proj",
    )(x, w)


def _mla_prep_body(p_ref, qn_ref, wq_ref, kvn_ref, wkv_ref, cos_ref, sin_ref, q_ref, k_ref, v_ref):
    p = p_ref[...]
    tm = p.shape[0]
    cos = cos_ref[...]
    sin = sin_ref[...]
    lane = lax.broadcasted_iota(jnp.int32, (tm, DH), 1)
    first_half = (lane & 31) < 16

    def rope(x):
        partner = jnp.where(first_half, pltpu.roll(x, DH - 16, 1), pltpu.roll(x, 16, 1))
        return x * cos + partner * sin

    cq = (_rms(p[:, :QL]) * qn_ref[...]).astype(BF16)
    ckv = (_rms(p[:, QL:QL + KVL]) * kvn_ref[...]).astype(BF16)
    kr = rope(p[:, QL + KVL:QL + KVL + DH]).astype(BF16)
    q = _dot(cq, wq_ref[...])
    kv = _dot(ckv, wkv_ref[...])
    for h in range(H):
        lo = 2 * DH * h
        q_ref[:, lo:lo + DH] = (q[:, lo:lo + DH] * Q_SCALE).astype(BF16)
        q_ref[:, lo + DH:lo + 2 * DH] = (rope(q[:, lo + DH:lo + 2 * DH]) * Q_SCALE).astype(BF16)
        k_ref[:, lo:lo + DH] = kv[:, DH * h:DH * (h + 1)].astype(BF16)
        k_ref[:, lo + DH:lo + 2 * DH] = kr
    v_ref[...] = kv[:, H * DH:].astype(BF16)


def mla_prep(p, q_norm, wq, kv_norm, wkv, cos, sin):
    m = p.shape[0]
    tm = 512
    row = lambda i: (i, 0)
    fixed = lambda i: (0, 0)
    return pl.pallas_call(
        _mla_prep_body,
        grid=(m // tm,),
        in_specs=[pl.BlockSpec((tm, P_FRONT), row),
                  pl.BlockSpec((1, QL), fixed),
                  pl.BlockSpec((QL, 2 * H * DH), fixed),
                  pl.BlockSpec((1, KVL), fixed),
                  pl.BlockSpec((KVL, 2 * H * DH), fixed),
                  pl.BlockSpec((tm, DH), row),
                  pl.BlockSpec((tm, DH), row)],
        out_specs=[pl.BlockSpec((tm, 2 * H * DH), row),
                   pl.BlockSpec((tm, 2 * H * DH), row),
                   pl.BlockSpec((tm, H * DH), row)],
        out_shape=[jax.ShapeDtypeStruct((m, 2 * H * DH), BF16),
                   jax.ShapeDtypeStruct((m, 2 * H * DH), BF16),
                   jax.ShapeDtypeStruct((m, H * DH), BF16)],
        compiler_params=_cp("parallel"),
        name="mla_prep",
    )(p, q_norm.reshape(1, QL), wq, kv_norm.reshape(1, KVL), wkv, cos, sin)


def _attn_lat_body(q_ref, kl_ref, vl_ref, kc_ref, vc_ref, o_ref):
    for hh in range(ATTN_HB):
        qk = slice(hh * 2 * DH, (hh + 1) * 2 * DH)
        vo = slice(hh * DH, (hh + 1) * DH)
        q = q_ref[:, qk]
        sl = _dot_nt(q, kl_ref[:, qk])
        sc = _dot_nt(q, kc_ref[:, qk])
        mx = jnp.maximum(jnp.max(sl, axis=-1, keepdims=True), jnp.max(sc, axis=-1, keepdims=True))
        el = jnp.exp2(sl - mx)
        ec = jnp.exp2(sc - mx)
        den = jnp.sum(el, axis=-1, keepdims=True) + jnp.sum(ec, axis=-1, keepdims=True)
        o = _dot(el.astype(BF16), vl_ref[:, vo]) + _dot(ec.astype(BF16), vc_ref[:, vo])
        o_ref[:, vo] = (o / den).astype(BF16)


def _attn_ctx_body(q_ref, kc_ref, vc_ref, o_ref):
    sc = _dot_nt(q_ref[...], kc_ref[...])
    ec = jnp.exp2(sc - jnp.max(sc, axis=-1, keepdims=True))
    o = _dot(ec.astype(BF16), vc_ref[...])
    o_ref[...] = (o / jnp.sum(ec, axis=-1, keepdims=True)).astype(BF16)


def attention_latent(q, k, v, b, t, tc):
    n_lat = b * t
    tq = 512
    nq = t // tq
    cb = n_lat // tc
    wqk = ATTN_HB * 2 * DH
    wv = ATTN_HB * DH
    return pl.pallas_call(
        _attn_lat_body,
        grid=(b, H // ATTN_HB, nq),
        in_specs=[pl.BlockSpec((tq, wqk), lambda bi, h, qi: (bi * nq + qi, h)),
                  pl.BlockSpec((t, wqk), lambda bi, h, qi: (bi, h)),
                  pl.BlockSpec((t, wv), lambda bi, h, qi: (bi, h)),
                  pl.BlockSpec((tc, wqk), lambda bi, h, qi: (cb + bi, h)),
                  pl.BlockSpec((tc, wv), lambda bi, h, qi: (cb + bi, h))],
        out_specs=pl.BlockSpec((tq, wv), lambda bi, h, qi: (bi * nq + qi, h)),
        out_shape=jax.ShapeDtypeStruct((n_lat, H * DH), BF16),
        compiler_params=_cp("parallel", "parallel", "parallel"),
        name="attn_latent",
    )(q, k, v, k, v)


def attention_context(q, k, v, b, t, tc):
    cb = (b * t) // tc
    return pl.pallas_call(
        _attn_ctx_body,
        grid=(b, H),
        in_specs=[pl.BlockSpec((tc, 2 * DH), lambda bi, h: (cb + bi, h)),
                  pl.BlockSpec((tc, 2 * DH), lambda bi, h: (cb + bi, h)),
                  pl.BlockSpec((tc, DH), lambda bi, h: (cb + bi, h))],
        out_specs=pl.BlockSpec((tc, DH), lambda bi, h: (bi, h)),
        out_shape=jax.ShapeDtypeStruct((b * tc, H * DH), BF16),
        compiler_params=_cp("parallel", "parallel"),
        name="attn_context",
    )(q, k, v)


def _gla_tables():
    c = GLA_C
    t = np.arange(c)[:, None]
    r = np.arange(c)[None, :]
    mats = []
    for lv in range(GLA_LEVELS):
        same = (t >> (lv + 1)) == (r >> (lv + 1))
        bt = (t >> lv) & 1
        br = (r >> lv) & 1
        upper = same & (bt == 1) & (br == 1) & (r <= t)
        lower = same & (bt == 0) & (br == 0) & (r > t)
        mats.append(upper | lower)
    mats.append(r <= t)
    mats.append(r > t)
    fwd = np.stack(mats).astype(np.float32)
    bwd = fwd[:, ::-1, ::-1]
    masks = np.stack([fwd, bwd]).reshape(2, (GLA_LEVELS + 2) * c, c)
    masks = np.concatenate([masks, masks], axis=2)
    x = t ^ r
    lvl = np.where(x == 0, -1, np.floor(np.log2(np.maximum(x, 1)))).astype(np.int32)
    return masks, lvl


def _gla_gates(x, lb_raw, layer):
    log_sig = jnp.minimum(x, 0.0) - jnp.log1p(jnp.exp(-jnp.abs(x)))
    if layer == 0:
        g = log_sig
        kk = 1.0 / (1.0 + jnp.exp(x))
    else:
        e = jnp.exp(lb_raw - jnp.max(lb_raw, axis=0, keepdims=True))
        prob = e / jnp.sum(e, axis=0, keepdims=True)
        lb = jnp.sum(prob[1:layer + 1], axis=0, keepdims=True)
        a = jnp.log(lb)
        bb = jnp.log1p(-lb) + log_sig
        g = jnp.maximum(a, bb) + jnp.log1p(jnp.exp(-jnp.abs(a - bb)))
        kk = (1.0 - lb) / (1.0 + jnp.exp(x))
    return g, kk


def _gla_chain(q, kk, v, ex, decay, level_is, row_bit, st_ref, slot, *, d):
    c = GLA_C
    v = v.astype(BF16)
    att = jnp.where(level_is[-1], _dot_nt(q.astype(BF16), kk.astype(BF16)), 0.0)
    for lv in range(GLA_LEVELS):
        ex_l = ex[lv * c:(lv + 1) * c]
        half = 1 << lv
        if half < SUBLANES:
            is_q = row_bit[lv] if d == 0 else jnp.logical_not(row_bit[lv])
            scaled = jnp.where(is_q, q, kk) * ex_l
            ql = jnp.where(is_q, scaled, 0.0)
            kl = jnp.where(is_q, 0.0, scaled)
            att = jnp.where(level_is[lv], _dot_nt(ql.astype(BF16), kl.astype(BF16)), att)
        else:
            zeros = jnp.zeros((half, DH), F32)
            blocks = [slice(blk * half, (blk + 1) * half) for blk in range(c // half)]
            q_side = [(blk & 1) == 1 - d for blk in range(c // half)]
            ql = jnp.concatenate([q[rows] * ex_l[rows] for rows, qs in zip(blocks, q_side) if qs], axis=0)
            kl = jnp.concatenate([zeros if qs else kk[rows] * ex_l[rows] for rows, qs in zip(blocks, q_side)], axis=0)
            prod = _dot_nt(ql.astype(BF16), kl.astype(BF16))
            parts, used = [], 0
            for rows, qs in zip(blocks, q_side):
                if qs:
                    parts.append(jnp.where(level_is[lv][rows], prod[used:used + half], att[rows]))
                    used += half
                else:
                    parts.append(att[rows])
            att = jnp.concatenate(parts, axis=0)

    q_in = (q * ex[GLA_LEVELS * c:(GLA_LEVELS + 1) * c]).astype(BF16)
    k_out = (kk * ex[(GLA_LEVELS + 1) * c:]).astype(BF16)
    st = st_ref[slot]
    o = _dot(att.astype(BF16), v) + _dot_nt(q_in, st.astype(BF16))
    st_ref[slot] = st * decay + _dot_tn(v, k_out)
    return o


def _gla_body(qf_ref, ff_ref, vf_ref, qb_ref, fb_ref, vb_ref, lb_ref, msk_ref, lvl_ref, of_ref, ob_ref, st_ref,
              *, layer):
    c = GLA_C

    @pl.when(pl.program_id(2) == 0)
    def _():
        st_ref[...] = jnp.zeros_like(st_ref)

    lvl = lvl_ref[...]
    level_is = [lvl == lv for lv in range(GLA_LEVELS)] + [lvl == -1]
    row = lax.broadcasted_iota(jnp.int32, (c, DH), 0)
    row_bit = [((row >> lv) & 1) == 1 for lv in range(GLA_LEVELS)]
    for d, (q_ref, f_ref, v_ref, o_ref) in enumerate(((qf_ref, ff_ref, vf_ref, of_ref),
                                                      (qb_ref, fb_ref, vb_ref, ob_ref))):
        g, kk = _gla_gates(f_ref[...], lb_ref[d], layer)
        g_hi = g.astype(BF16)
        g_lo = (g - g_hi.astype(F32)).astype(BF16)
        ex = jnp.exp(_dot(msk_ref[d], jnp.concatenate([g_hi, g_lo], axis=0)))
        decay = jnp.exp(jnp.sum(g, axis=0, keepdims=True))
        for hh in range(GLA_HB):
            cols = slice(hh * DH, (hh + 1) * DH)
            o_ref[:, cols] = _gla_chain(q_ref[:, cols], kk[:, cols], v_ref[:, cols], ex[:, cols], decay[:, cols],
                                        level_is, row_bit, st_ref, d * GLA_HB + hh, d=d)


def gla(p, lb_raw, masks, lvl, b, t, tc, layer):
    c = GLA_C
    m = p.shape[0]
    n_ctx_chunks = tc // c
    n_lat_chunks = t // c
    n_chunks = n_ctx_chunks + n_lat_chunks
    ctx0 = (b * t) // c
    n_even = lb_raw.shape[1]
    wb = GLA_HB * DH

    def blk(bi, i, d):
        jc = i if d == 0 else n_ctx_chunks - 1 - i
        jl = i - n_ctx_chunks if d == 0 else n_chunks - 1 - i
        return jnp.where(i < n_ctx_chunks, ctx0 + bi * n_ctx_chunks + jc, bi * n_lat_chunks + jl)

    def spec(section, d):
        col0 = (P_FRONT + section * H * DH) // wb
        return pl.BlockSpec((c, wb), lambda bi, hh, i: (blk(bi, i, d), col0 + hh))

    out_spec = lambda d: pl.BlockSpec((c, wb), lambda bi, hh, i: (blk(bi, i, d), hh))
    return pl.pallas_call(
        functools.partial(_gla_body, layer=layer),
        grid=(b, H // GLA_HB, n_chunks),
        in_specs=[spec(0, 0), spec(1, 0), spec(3, 0), spec(0, 1), spec(2, 1), spec(3, 1),
                  pl.BlockSpec((2, n_even, wb), lambda bi, hh, i: (0, 0, hh)),
                  pl.BlockSpec((2, (GLA_LEVELS + 2) * c, 2 * c), lambda bi, hh, i: (0, 0, 0)),
                  pl.BlockSpec((c, c), lambda bi, hh, i: (0, 0))],
        out_specs=[out_spec(0), out_spec(1)],
        out_shape=[jax.ShapeDtypeStruct((m, H * DH), F32)] * 2,
        scratch_shapes=[pltpu.VMEM((2 * GLA_HB, DH, DH), F32)],
        compiler_params=_cp("parallel", "parallel", "arbitrary"),
        name="hgrn_scan",
    )(p, p, p, p, p, p, lb_raw, masks, lvl)


def _hg_merge_body(of_ref, ob_ref, g_ref, n_ref, y_ref):
    o = of_ref[...] + ob_ref[...]
    gate = g_ref[...]
    gain = n_ref[...]
    for h in range(H):
        sl = slice(h * DH, (h + 1) * DH)
        gh = gate[:, sl]
        y_ref[:, sl] = (_rms(o[:, sl]) * gain * (gh * _sigmoid(gh))).astype(BF16)


def hg_merge(o_f, o_b, p, hg_norm):
    m = p.shape[0]
    tm = 512
    w = H * DH
    row = lambda i: (i, 0)
    return pl.pallas_call(
        _hg_merge_body,
        grid=(m // tm,),
        in_specs=[pl.BlockSpec((tm, w), row),
                  pl.BlockSpec((tm, w), row),
                  pl.BlockSpec((tm, w), lambda i: (i, (P_WIDTH - w) // w)),
                  pl.BlockSpec((1, DH), lambda i: (0, 0))],
        out_specs=pl.BlockSpec((tm, w), row),
        out_shape=jax.ShapeDtypeStruct((m, w), BF16),
        compiler_params=_cp("parallel"),
        name="hgrn_merge",
    )(o_f, o_b, p, hg_norm.reshape(1, DH))


RES_TM = 512
RES_TN = 512
RES_ROWS = 128


def _mm_res_body(*refs, groups, gate, shift, scale, nj, has_bias, has_next):
    refs = list(refs)
    a_refs = [[refs.pop(0) for _ in pieces] for _, pieces in groups]
    w_ref = refs.pop(0)
    b_ref = refs.pop(0) if has_bias else None
    x_ref, gp_ref, mc_ref = refs.pop(0), refs.pop(0), refs.pop(0)
    gn_ref, mn_ref = (refs.pop(0), refs.pop(0)) if has_next else (None, None)
    xo_ref = refs.pop(0)
    uo_ref = refs.pop(0) if has_next else None
    i = pl.program_id(0)
    j = pl.program_id(1)
    tn = w_ref.shape[1]
    cols = pl.ds(pl.multiple_of(j * tn, tn), tn)

    w = w_ref[...]
    koff = 0
    for gi, ((kw, pieces), piece_refs) in enumerate(zip(groups, a_refs)):
        wk = w[koff:koff + kw]
        koff += kw
        for a_ref, (r0, nr) in zip(piece_refs, pieces):
            def project(a_ref=a_ref, wk=wk, gi=gi):
                part = _dot(a_ref[...], wk)
                if gi == 0:
                    xo_ref[:, cols] = part
                else:
                    xo_ref[:, cols] += part

            if len(pieces) == 1:
                project()
            else:
                pl.when((i >= r0) & (i < r0 + nr))(project)

    @pl.when(j == nj - 1)
    def _():
        mc = mc_ref[0]
        gate_gain = mc[:, gate * D:(gate + 1) * D] * gp_ref[...]
        if has_next:
            mn = mn_ref[0]
            scale_gain = gn_ref[...] * (1.0 + mn[:, scale * D:(scale + 1) * D])
            shift_v = mn[:, shift * D:(shift + 1) * D]

        def rows_pass(r, carry):
            rows = pl.ds(pl.multiple_of(r * RES_ROWS, RES_ROWS), RES_ROWS)
            y = xo_ref[rows, :]
            if has_bias:
                y = y + b_ref[...]
            xn = x_ref[rows, :] + _rms(y) * gate_gain
            xo_ref[rows, :] = xn
            if has_next:
                uo_ref[rows, :] = (_rms(xn) * scale_gain + shift_v).astype(BF16)
            return carry

        lax.fori_loop(0, xo_ref.shape[0] // RES_ROWS, rows_pass, 0)


def matmul_residual(groups, w, layer, bias, x, m, g_post, mods_cur, gate, n_lat, seq, tm, nxt=None):
    tn = RES_TN
    nj = D // tn
    kdim = w.shape[1]
    mrow = lambda i, j: (_mod_row(i, tm, n_lat, seq), 0, 0)
    row = lambda i, j: (i, 0)
    fixed = lambda i, j: (0, 0)

    def piece_spec(arr, r0, nr):
        return pl.BlockSpec((tm, arr.shape[1]), lambda i, j: (jnp.clip(i - r0, 0, nr - 1), 0))

    in_specs = [piece_spec(*pc) for g in groups for pc in g]
    in_specs.append(pl.BlockSpec((None, kdim, tn), lambda i, j: (layer, 0, j)))
    args = [pc[0] for g in groups for pc in g] + [w]
    if bias is not None:
        in_specs.append(pl.BlockSpec((1, D), fixed))
        args.append(bias.reshape(1, D))
    in_specs += [pl.BlockSpec((tm, D), row),
                 pl.BlockSpec((1, D), fixed),
                 pl.BlockSpec((1, 1, 6 * D), mrow)]
    args += [x, g_post.reshape(1, D), mods_cur]
    out_specs = [pl.BlockSpec((tm, D), row)]
    out_shape = [jax.ShapeDtypeStruct((m, D), F32)]
    shift = scale = 0
    if nxt is not None:
        g_next, mods_next, shift, scale = nxt
        in_specs += [pl.BlockSpec((1, D), fixed), pl.BlockSpec((1, 1, 6 * D), mrow)]
        args += [g_next.reshape(1, D), mods_next]
        out_specs.append(pl.BlockSpec((tm, D), row))
        out_shape.append(jax.ShapeDtypeStruct((m, D), BF16))
    group_meta = tuple((g[0][0].shape[1], tuple(pc[1:] for pc in g)) for g in groups)
    assert sum(kw for kw, _ in group_meta) == kdim
    outs = pl.pallas_call(
        functools.partial(_mm_res_body, groups=group_meta, gate=gate, shift=shift, scale=scale,
                          nj=nj, has_bias=bias is not None, has_next=nxt is not None),
        grid=(m // tm, nj),
        in_specs=in_specs,
        out_specs=out_specs,
        out_shape=out_shape,
        compiler_params=_cp("parallel", "arbitrary"),
        name="proj_residual",
    )(*args)
    return (outs[0], outs[1]) if nxt is not None else (outs[0], None)


GATED_HALO = 16


def _mm_gated_body(*refs, conv, silu, lat_tiles, seq_lat, seq_ctx):
    if conv:
        x_ref, xp_ref, xn_ref, wa_ref, wg_ref, ca_ref, cg_ref, ba_ref, bg_ref, o_ref = refs
    else:
        x_ref, wa_ref, wg_ref, ba_ref, bg_ref, o_ref = refs
    x = x_ref[...]
    wa = wa_ref[...].astype(BF16)
    wg = wg_ref[...].astype(BF16)
    za = _dot(x, wa)
    zg = _dot(x, wg)
    if conv:
        i = pl.program_id(0)
        tm, tn = za.shape
        halo = jnp.concatenate([xp_ref[...], xn_ref[...]], axis=0)
        ha = _dot(halo, wa)
        hg = _dot(halo, wg)
        is_ctx = i >= lat_tiles
        seq_m1 = jnp.where(is_ctx, seq_ctx - 1, seq_lat - 1)
        starts_seq = ((i * tm) & seq_m1) == 0
        ends_seq = (((i + 1) * tm) & seq_m1) == 0
        sub = lax.broadcasted_iota(jnp.int32, (SUBLANES, tn), 0)
        top = sub == 0
        bottom = sub == SUBLANES - 1
        inner = range(seq_ctx, tm, seq_ctx)
        g = SUBLANES

        def conv3(z, h, cw):
            up = pltpu.roll(z, 1, 0)
            dn = pltpu.roll(z, tm - 1, 0)
            before = jnp.where(starts_seq, 0.0, h[GATED_HALO - 1:GATED_HALO])
            after = jnp.where(ends_seq, 0.0, h[GATED_HALO:GATED_HALO + 1])
            up_parts, dn_parts = [jnp.where(top, before, up[:g])], []
            up_lo, dn_lo = g, 0
            for r in inner:
                up_parts += [up[up_lo:r], jnp.where(jnp.logical_and(top, is_ctx), 0.0, up[r:r + g])]
                dn_parts += [dn[dn_lo:r - g], jnp.where(jnp.logical_and(bottom, is_ctx), 0.0, dn[r - g:r])]
                up_lo, dn_lo = r + g, r
            up_parts.append(up[up_lo:])
            dn_parts += [dn[dn_lo:tm - g], jnp.where(bottom, after, dn[tm - g:])]
            up = jnp.concatenate(up_parts, axis=0)
            dn = jnp.concatenate(dn_parts, axis=0)
            return cw[0:1] * up + cw[1:2] * z + cw[2:3] * dn

        za = conv3(za, ha, ca_ref[...])
        zg = conv3(zg, hg, cg_ref[...])
    za = za + ba_ref[...]
    zg = zg + bg_ref[...]
    sg = _sigmoid(zg)
    o_ref[...] = (za * (zg * sg if silu else sg)).astype(o_ref.dtype)


def matmul_gated(x, w, layer, conv_w, bias, *, m, tm, tn, n_lat, seq_lat, seq_ctx, silu, out_dtype, name):
    kdim = x.shape[1]
    n = w.shape[2] // 2
    nj = n // tn
    first = lambda i, j: (layer, 0, j)
    second = lambda i, j: (layer, 0, j + nj)
    bias = bias.reshape(bias.shape[0], 1, 2 * n)
    in_specs = [pl.BlockSpec((tm, kdim), lambda i, j: (i, 0))]
    args = [x]
    if conv_w is not None:
        hb = tm // GATED_HALO
        last_hb = x.shape[0] // GATED_HALO - 1
        in_specs += [pl.BlockSpec((GATED_HALO, kdim), lambda i, j: (jnp.maximum(i * hb - 1, 0), 0)),
                     pl.BlockSpec((GATED_HALO, kdim), lambda i, j: (jnp.minimum((i + 1) * hb, last_hb), 0))]
        args += [x, x]
    in_specs += [pl.BlockSpec((None, kdim, tn), first), pl.BlockSpec((None, kdim, tn), second)]
    args += [w, w]
    if conv_w is not None:
        in_specs += [pl.BlockSpec((None, conv_w.shape[1], tn), first),
                     pl.BlockSpec((None, conv_w.shape[1], tn), second)]
        args += [conv_w, conv_w]
    in_specs += [pl.BlockSpec((None, 1, tn), first), pl.BlockSpec((None, 1, tn), second)]
    args += [bias, bias]
    return pl.pallas_call(
        functools.partial(_mm_gated_body, conv=conv_w is not None, silu=silu, lat_tiles=n_lat // tm,
                          seq_lat=seq_lat, seq_ctx=seq_ctx),
        grid=(m // tm, nj),
        in_specs=in_specs,
        out_specs=pl.BlockSpec((tm, tn), lambda i, j: (i, j)),
        out_shape=jax.ShapeDtypeStruct((m, n), out_dtype),
        compiler_params=_cp("parallel", "parallel"),
        name=name,
    )(*args)


def _dwconv_ln_body(prev_ref, cur_ref, next_ref, w_ref, b_ref, g_ref, beta_ref, o_ref, ext_ref, sh_ref, acc_ref,
                    *, tiles_per_seq, n_lat_tiles):
    i = pl.program_id(0)
    tm = cur_ref.shape[0]
    hw = CONV_HALO
    is_lat = i < n_lat_tiles
    j = i % tiles_per_seq
    has_prev = jnp.logical_and(is_lat, j > 0).astype(F32)
    has_next = jnp.logical_and(is_lat, j < tiles_per_seq - 1).astype(F32)
    ext_ref[0:hw, :] = prev_ref[tm - hw:, :] * has_prev
    ext_ref[hw:hw + tm, :] = cur_ref[...]
    ext_ref[hw + tm:, :] = next_ref[0:hw, :] * has_next

    sh_rows = sh_ref.shape[1]

    def lane_block(cb, carry):
        col = pl.multiple_of(cb * DH, DH)
        wv = w_ref[:, pl.ds(col, DH)]
        for s in range(SUBLANES):
            sh_ref[s] = ext_ref[s:s + sh_rows, pl.ds(col, DH)]
        for r0 in range(0, tm, 128):
            acc = jnp.zeros((128, DH), F32)
            for k in range(CONV_W):
                off = r0 + k + hw - (CONV_W - 1) // 2
                s = off % SUBLANES
                acc = acc + sh_ref[s, off - s:off - s + 128, :] * wv[k:k + 1]
            acc_ref[r0:r0 + 128, pl.ds(col, DH)] = acc
        return carry

    lax.fori_loop(0, D // DH, lane_block, 0)
    hcv = acc_ref[...] + b_ref[...]
    mu = jnp.mean(hcv, axis=-1, keepdims=True)
    xc = hcv - mu
    var = jnp.mean(xc * xc, axis=-1, keepdims=True)
    y = xc * lax.rsqrt(var + EPS) * g_ref[...] + beta_ref[...]
    o_ref[...] = (y * _sigmoid(y)).astype(BF16)


def dwconv_ln(hx, w_dw, b_dw, ln_g, ln_b, n_lat, seq):
    m = hx.shape[0]
    tm = 256
    nt = m // tm
    fixed = lambda i: (0, 0)
    vec = lambda a: a.reshape(1, D)
    return pl.pallas_call(
        functools.partial(_dwconv_ln_body, tiles_per_seq=seq // tm, n_lat_tiles=n_lat // tm),
        grid=(nt,),
        in_specs=[pl.BlockSpec((tm, D), lambda i: (jnp.maximum(i - 1, 0), 0)),
                  pl.BlockSpec((tm, D), lambda i: (i, 0)),
                  pl.BlockSpec((tm, D), lambda i: (jnp.minimum(i + 1, nt - 1), 0)),
                  pl.BlockSpec((CONV_W, D), fixed),
                  pl.BlockSpec((1, D), fixed),
                  pl.BlockSpec((1, D), fixed),
                  pl.BlockSpec((1, D), fixed)],
        out_specs=pl.BlockSpec((tm, D), lambda i: (i, 0)),
        out_shape=jax.ShapeDtypeStruct((m, D), BF16),
        scratch_shapes=[pltpu.VMEM((tm + 2 * CONV_HALO, D), F32),
                        pltpu.VMEM((SUBLANES, tm + 2 * CONV_HALO - SUBLANES, DH), F32),
                        pltpu.VMEM((tm, D), F32)],
        compiler_params=_cp("parallel"),
        name="dwconv_ln",
    )(hx, hx, hx, w_dw, vec(b_dw), vec(ln_g), vec(ln_b))


def _rope_tables(b, t, tc):
    rows = t // GRID_W
    row = jnp.repeat(jnp.arange(rows), GRID_W).astype(F32)
    col = jnp.tile(jnp.arange(GRID_W), rows).astype(F32)
    pairs = ROPE // 4
    freqs = ROPE_BASE ** (-jnp.arange(pairs, dtype=F32) / pairs)
    ar = row[:, None] * freqs
    ac = col[:, None] * freqs
    cos = jnp.concatenate([jnp.cos(ar), jnp.cos(ar), jnp.cos(ac), jnp.cos(ac), jnp.ones((t, DH - ROPE), F32)], axis=1)
    sin = jnp.concatenate([-jnp.sin(ar), jnp.sin(ar), -jnp.sin(ac), jnp.sin(ac), jnp.zeros((t, DH - ROPE), F32)], axis=1)
    cos = jnp.concatenate([jnp.tile(cos, (b, 1)), jnp.ones((b * tc, DH), F32)], axis=0)
    sin = jnp.concatenate([jnp.tile(sin, (b, 1)), jnp.zeros((b * tc, DH), F32)], axis=0)
    return cos, sin


def _pick(n, *cands):
    for cand in cands:
        if n % cand == 0:
            return cand
    raise ValueError(f"no tile for {n}")


def kernel(x, c, ctx, c_ctx, w_mod, b_mod, norm_gains, w_in_ab, mla_q_norm, w_q_up, mla_kv_norm, w_kv_up, hgrn_lb, hgrn_norm, w_out_ab, conv_w_pw1, conv_b_pw1, conv_w_dw, conv_b_dw, conv_ln_g, conv_ln_b, conv_w_pw2, conv_b_pw2, ffn_w_up, ffn_w_conv, ffn_b_conv, ffn_w_down):
    b, t, _ = x.shape
    tc = ctx.shape[1]
    depth = w_mod.shape[0]
    n_lat = b * t
    n_ctx = b * tc
    assert x.shape[2] == D and depth % 2 == 0 and b < 8
    assert t % 512 == 0 and tc % 256 == 0 and n_ctx % 512 == 0 and t % n_ctx == 0
    assert t & (t - 1) == 0 and tc & (tc - 1) == 0
    tg = _pick(n_ctx, 1024, 512)
    tr = RES_TM
    lat_tiles = n_lat // tr
    ctx_tiles = n_ctx // tr
    w_out_bf, w_pw2_bf, w_down_bf = w_out_ab.astype(BF16), conv_w_pw2.astype(BF16), ffn_w_down.astype(BF16)

    hcur = jnp.concatenate([x.reshape(n_lat, D), ctx.reshape(n_ctx, D)], axis=0)
    cc = jnp.concatenate([c, c_ctx[None, :], jnp.zeros((7 - b, D), F32)], axis=0)
    mods = modulation(cc, w_mod, b_mod).reshape(depth, 8, 1, 6 * D)
    cos, sin = _rope_tables(b, t, tc)
    masks_np, lvl_np = _gla_tables()
    masks = jnp.asarray(masks_np, BF16)
    lvl = jnp.asarray(lvl_np)

    u = norm_mod(hcur, norm_gains[0, 0], mods[0], n_lat, t, shift=0, scale=1)
    for l in range(depth):
        last = l == depth - 1
        j = l // 2
        m = n_lat if last else n_lat + n_ctx
        row_tiles = m // tr
        g_pre1, g_post1, g_pre2, g_post2 = norm_gains[l]
        if l % 2 == 0:
            w_in = w_in_ab[j]
            w_in_p = jnp.concatenate([w_in[:, :QL + KVL + ROPE], jnp.zeros((D, P_FRONT - QL - KVL - ROPE), F32),
                                      w_in[:, QL + KVL + ROPE:]], axis=1).astype(BF16)
            wq = jnp.pad(w_q_up[j].reshape(QL, H, DH + ROPE), ((0, 0), (0, 0), (0, DH - ROPE)))
            wq = wq.reshape(QL, 2 * H * DH).astype(BF16)
            wkv = w_kv_up[j].reshape(KVL, H, 2 * DH)
            wkv = jnp.concatenate([wkv[:, :, :DH].reshape(KVL, H * DH), wkv[:, :, DH:].reshape(KVL, H * DH)], axis=1)
            p = matmul(u, w_in_p, _pick(m, 1024, 512), 512)
            q, k, v = mla_prep(p, mla_q_norm[j], wq, mla_kv_norm[j], wkv.astype(BF16), cos, sin)
            att_lat = attention_latent(q, k, v, b, t, tc)
            att_ctx = attention_context(q, k, v, b, t, tc)
            o_f, o_b = gla(p, hgrn_lb, masks, lvl, b, t, tc, j)
            hg = hg_merge(o_f, o_b, p, hgrn_norm[j])
            att_pieces = [(att_lat, 0, lat_tiles)] + ([] if last else [(att_ctx, lat_tiles, ctx_tiles)])
            groups = [att_pieces, [(hg, 0, row_tiles)]]
            w_o, b_o = w_out_bf, None
        else:
            glu = matmul_gated(u, conv_w_pw1, j, None, conv_b_pw1, m=m, tm=tg, tn=512, n_lat=n_lat, seq_lat=t,
                               seq_ctx=tc, silu=False, out_dtype=F32, name="conformer_glu")
            mix = dwconv_ln(glu, conv_w_dw[j], conv_b_dw[j], conv_ln_g[j], conv_ln_b[j], n_lat, t)
            groups = [[(mix, 0, row_tiles)]]
            w_o, b_o = w_pw2_bf, conv_b_pw2[j]
        hcur, u = matmul_residual(groups, w_o, j, b_o, hcur, m, g_post1, mods[l], 2, n_lat, t, tr,
                                  nxt=(g_pre2, mods[l], 3, 4))
        act = matmul_gated(u, ffn_w_up, l, ffn_w_conv, ffn_b_conv, m=m, tm=tg, tn=512, n_lat=n_lat, seq_lat=t,
                           seq_ctx=tc, silu=True, out_dtype=BF16, name="ffn_up")
        nxt = None if last else (norm_gains[l + 1, 0], mods[l + 1], 0, 1)
        hcur, u = matmul_residual([[(act, 0, row_tiles)]], w_down_bf, l, None, hcur, m, g_post2, mods[l], 5,
                                  n_lat, t, tr, nxt=nxt)
    return hcur.reshape(b, t, D)
```

```python
import functools

import numpy as np
import jax
import jax.numpy as jnp
from jax import lax
from jax.experimental import pallas as pl
from jax.experimental.pallas import tpu as pltpu

F32 = jnp.float32
BF16 = jnp.bfloat16

D = 2048
H = D // 256
DH = 128
ROPE = 64
QL = D // 4
KVL = D // 8
DFF = (11 * D) // 4
GRID_W = 64
ROPE_BASE = 10000.0
CONV_W = 31
CONV_HALO = 16
EPS = 1e-6
ATTN_SCALE = (DH + ROPE) ** -0.5
Q_SCALE = ATTN_SCALE * float(np.log2(np.e))
ATTN_HB = 4
P_FRONT = 1024
P_WIDTH = P_FRONT + 5 * H * DH
GLA_C = 128
GLA_LEVELS = 7
GLA_HB = 8
SUBLANES = 8
VMEM_LIMIT = 56 * 1024 * 1024


def _cp(*sem):
    return pltpu.CompilerParams(dimension_semantics=sem, vmem_limit_bytes=VMEM_LIMIT)


def _sigmoid(x):
    return 1.0 / (1.0 + jnp.exp(-x))


def _rms(x):
    return x * lax.rsqrt(jnp.mean(x * x, axis=-1, keepdims=True) + EPS)


def _dot(a, b):
    return jnp.dot(a, b, preferred_element_type=F32)


def _dot_nt(a, b):
    return lax.dot_general(a, b, (((1,), (1,)), ((), ())), preferred_element_type=F32)


def _dot_tn(a, b):
    return lax.dot_general(a, b, (((0,), (0,)), ((), ())), preferred_element_type=F32)


def _mod_row(i, tm, n_lat, seq):
    return jnp.where(i < n_lat // tm, i // (seq // tm), n_lat // seq)


def _mod_body(c_ref, w_ref, b_ref, o_ref):
    c = c_ref[...]
    s = (c * _sigmoid(c)).astype(BF16)
    o_ref[0] = _dot(s, w_ref[0].astype(BF16)) + b_ref[0]


def modulation(cc, w_mod, b_mod):
    depth, _, n = w_mod.shape
    rows = cc.shape[0]
    tn = 1024
    return pl.pallas_call(
        _mod_body,
        grid=(depth, n // tn),
        in_specs=[pl.BlockSpec((rows, D), lambda l, j: (0, 0)),
                  pl.BlockSpec((1, D, tn), lambda l, j: (l, 0, j)),
                  pl.BlockSpec((1, 1, tn), lambda l, j: (l, 0, j))],
        out_specs=pl.BlockSpec((1, rows, tn), lambda l, j: (l, 0, j)),
        out_shape=jax.ShapeDtypeStruct((depth, rows, n), F32),
        compiler_params=_cp("parallel", "parallel"),
        name="modulation",
    )(cc, w_mod, b_mod.reshape(depth, 1, n))


def _norm_mod_body(x_ref, g_ref, m_ref, o_ref, *, shift, scale):
    m = m_ref[0]
    y = _rms(x_ref[...]) * g_ref[...]
    o_ref[...] = (y * (1.0 + m[:, scale * D:(scale + 1) * D]) + m[:, shift * D:(shift + 1) * D]).astype(BF16)


def norm_mod(x, gain, mods, n_lat, seq, shift, scale):
    m = x.shape[0]
    tm = 256
    return pl.pallas_call(
        functools.partial(_norm_mod_body, shift=shift, scale=scale),
        grid=(m // tm,),
        in_specs=[pl.BlockSpec((tm, D), lambda i: (i, 0)),
                  pl.BlockSpec((1, D), lambda i: (0, 0)),
                  pl.BlockSpec((1, 1, 6 * D), lambda i: (_mod_row(i, tm, n_lat, seq), 0, 0))],
        out_specs=pl.BlockSpec((tm, D), lambda i: (i, 0)),
        out_shape=jax.ShapeDtypeStruct((m, D), BF16),
        compiler_params=_cp("parallel"),
        name="norm_mod",
    )(x, gain.reshape(1, D), mods)


def _repack_in_body(w_ref, o_ref):
    front = QL + KVL + ROPE
    w = w_ref[...]
    o_ref[:, :front] = w[:, :front].astype(BF16)
    o_ref[:, front:P_FRONT] = jnp.zeros((w.shape[0], P_FRONT - front), BF16)
    o_ref[:, P_FRONT:] = w[:, front:].astype(BF16)


def repack_in_proj(w_in):
    layers, k, n = w_in.shape
    tr = 256
    return pl.pallas_call(
        _repack_in_body,
        grid=(layers, k // tr),
        in_specs=[pl.BlockSpec((None, tr, n), lambda l, i: (l, i, 0))],
        out_specs=pl.BlockSpec((None, tr, P_WIDTH), lambda l, i: (l, i, 0)),
        out_shape=jax.ShapeDtypeStruct((layers, k, P_WIDTH), BF16),
        compiler_params=_cp("parallel", "parallel"),
        name="repack_in_proj",
    )(w_in)


def _mm_body(x_ref, w_ref, o_ref):
    o_ref[...] = _dot(x_ref[...], w_ref[...])


def matmul(x, w, layer, tm, tn):
    m, k = x.shape
    n = w.shape[2]
    return pl.pallas_call(
        _mm_body,
        grid=(m // tm, n // tn),
        in_specs=[pl.BlockSpec((tm, k), lambda i, j: (i, 0)),
                  pl.BlockSpec((None, k, tn), lambda i, j: (layer, 0, j))],
        out_specs=pl.BlockSpec((tm, tn), lambda i, j: (i, j)),
        out_shape=jax.ShapeDtypeStruct((m, n), F32),
        compiler_params=_cp("parallel", "parallel"),
        name="in_proj",
    )(x, w)


def _mla_prep_body(p_ref, qn_ref, wq_ref, kvn_ref, wkv_ref, cos_ref, sin_ref, q_ref, k_ref, v_ref):
    p = p_ref[...]
    tm = p.shape[0]
    cos = cos_ref[...]
    sin = sin_ref[...]
    lane = lax.broadcasted_iota(jnp.int32, (tm, DH), 1)
    first_half = (lane & 31) < 16

    def rope(x):
        partner = jnp.where(first_half, pltpu.roll(x, DH - 16, 1), pltpu.roll(x, 16, 1))
        return x * cos + partner * sin

    cq = (_rms(p[:, :QL]) * qn_ref[...]).astype(BF16)
    ckv = (_rms(p[:, QL:QL + KVL]) * kvn_ref[...]).astype(BF16)
    kr = rope(p[:, QL + KVL:QL + KVL + DH]).astype(BF16)
    q = _dot(cq, wq_ref[...])
    kv = _dot(ckv, wkv_ref[...])
    for h in range(H):
        lo = 2 * DH * h
        q_ref[:, lo:lo + DH] = (q[:, lo:lo + DH] * Q_SCALE).astype(BF16)
        q_ref[:, lo + DH:lo + 2 * DH] = (rope(q[:, lo + DH:lo + 2 * DH]) * Q_SCALE).astype(BF16)
        k_ref[:, lo:lo + DH] = kv[:, DH * h:DH * (h + 1)].astype(BF16)
        k_ref[:, lo + DH:lo + 2 * DH] = kr
    v_ref[...] = kv[:, H * DH:].astype(BF16)


def mla_prep(p, q_norm, wq, kv_norm, wkv, cos, sin):
    m = p.shape[0]
    tm = 512
    row = lambda i: (i, 0)
    fixed = lambda i: (0, 0)
    return pl.pallas_call(
        _mla_prep_body,
        grid=(m // tm,),
        in_specs=[pl.BlockSpec((tm, P_FRONT), row),
                  pl.BlockSpec((1, QL), fixed),
                  pl.BlockSpec((QL, 2 * H * DH), fixed),
                  pl.BlockSpec((1, KVL), fixed),
                  pl.BlockSpec((KVL, 2 * H * DH), fixed),
                  pl.BlockSpec((tm, DH), row),
                  pl.BlockSpec((tm, DH), row)],
        out_specs=[pl.BlockSpec((tm, 2 * H * DH), row),
                   pl.BlockSpec((tm, 2 * H * DH), row),
                   pl.BlockSpec((tm, H * DH), row)],
        out_shape=[jax.ShapeDtypeStruct((m, 2 * H * DH), BF16),
                   jax.ShapeDtypeStruct((m, 2 * H * DH), BF16),
                   jax.ShapeDtypeStruct((m, H * DH), BF16)],
        compiler_params=_cp("parallel"),
        name="mla_prep",
    )(p, q_norm.reshape(1, QL), wq, kv_norm.reshape(1, KVL), wkv, cos, sin)


def _attn_lat_body(q_ref, kl_ref, vl_ref, kc_ref, vc_ref, o_ref):
    for hh in range(ATTN_HB):
        qk = slice(hh * 2 * DH, (hh + 1) * 2 * DH)
        vo = slice(hh * DH, (hh + 1) * DH)
        q = q_ref[:, qk]
        sl = _dot_nt(q, kl_ref[:, qk])
        sc = _dot_nt(q, kc_ref[:, qk])
        mx = jnp.maximum(jnp.max(sl, axis=-1, keepdims=True), jnp.max(sc, axis=-1, keepdims=True))
        el = jnp.exp2(sl - mx)
        ec = jnp.exp2(sc - mx)
        den = jnp.sum(el, axis=-1, keepdims=True) + jnp.sum(ec, axis=-1, keepdims=True)
        o = _dot(el.astype(BF16), vl_ref[:, vo]) + _dot(ec.astype(BF16), vc_ref[:, vo])
        o_ref[:, vo] = (o / den).astype(BF16)


def _attn_ctx_body(q_ref, kc_ref, vc_ref, o_ref):
    sc = _dot_nt(q_ref[...], kc_ref[...])
    ec = jnp.exp2(sc - jnp.max(sc, axis=-1, keepdims=True))
    o = _dot(ec.astype(BF16), vc_ref[...])
    o_ref[...] = (o / jnp.sum(ec, axis=-1, keepdims=True)).astype(BF16)


def attention_latent(q, k, v, b, t, tc):
    n_lat = b * t
    tq = 512
    nq = t // tq
    cb = n_lat // tc
    wqk = ATTN_HB * 2 * DH
    wv = ATTN_HB * DH
    return pl.pallas_call(
        _attn_lat_body,
        grid=(b, H // ATTN_HB, nq),
        in_specs=[pl.BlockSpec((tq, wqk), lambda bi, h, qi: (bi * nq + qi, h)),
                  pl.BlockSpec((t, wqk), lambda bi, h, qi: (bi, h)),
                  pl.BlockSpec((t, wv), lambda bi, h, qi: (bi, h)),
                  pl.BlockSpec((tc, wqk), lambda bi, h, qi: (cb + bi, h)),
                  pl.BlockSpec((tc, wv), lambda bi, h, qi: (cb + bi, h))],
        out_specs=pl.BlockSpec((tq, wv), lambda bi, h, qi: (bi * nq + qi, h)),
        out_shape=jax.ShapeDtypeStruct((n_lat, H * DH), BF16),
        compiler_params=_cp("parallel", "parallel", "parallel"),
        name="attn_latent",
    )(q, k, v, k, v)


def attention_context(q, k, v, b, t, tc):
    cb = (b * t) // tc
    return pl.pallas_call(
        _attn_ctx_body,
        grid=(b, H),
        in_specs=[pl.BlockSpec((tc, 2 * DH), lambda bi, h: (cb + bi, h)),
                  pl.BlockSpec((tc, 2 * DH), lambda bi, h: (cb + bi, h)),
                  pl.BlockSpec((tc, DH), lambda bi, h: (cb + bi, h))],
        out_specs=pl.BlockSpec((tc, DH), lambda bi, h: (bi, h)),
        out_shape=jax.ShapeDtypeStruct((b * tc, H * DH), BF16),
        compiler_params=_cp("parallel", "parallel"),
        name="attn_context",
    )(q, k, v)


def _gla_tables():
    c = GLA_C
    t = np.arange(c)[:, None]
    r = np.arange(c)[None, :]
    mats = []
    for lv in range(GLA_LEVELS):
        same = (t >> (lv + 1)) == (r >> (lv + 1))
        bt = (t >> lv) & 1
        br = (r >> lv) & 1
        upper = same & (bt == 1) & (br == 1) & (r <= t)
        lower = same & (bt == 0) & (br == 0) & (r > t)
        mats.append(upper | lower)
    mats.append(r <= t)
    mats.append(r > t)
    fwd = np.stack(mats).astype(np.float32)
    bwd = fwd[:, ::-1, ::-1]
    masks = np.stack([fwd, bwd]).reshape(2, (GLA_LEVELS + 2) * c, c)
    masks = np.concatenate([masks, masks], axis=2)
    x = t ^ r
    lvl = np.where(x == 0, -1, np.floor(np.log2(np.maximum(x, 1)))).astype(np.int32)
    return masks, lvl


def _gla_gates(x, lb_raw, layer):
    log_sig = jnp.minimum(x, 0.0) - jnp.log1p(jnp.exp(-jnp.abs(x)))
    if layer == 0:
        g = log_sig
        kk = 1.0 / (1.0 + jnp.exp(x))
    else:
        e = jnp.exp(lb_raw - jnp.max(lb_raw, axis=0, keepdims=True))
        prob = e / jnp.sum(e, axis=0, keepdims=True)
        lb = jnp.sum(prob[1:layer + 1], axis=0, keepdims=True)
        a = jnp.log(lb)
        bb = jnp.log1p(-lb) + log_sig
        g = jnp.maximum(a, bb) + jnp.log1p(jnp.exp(-jnp.abs(a - bb)))
        kk = (1.0 - lb) / (1.0 + jnp.exp(x))
    return g, kk


def _gla_chain(q, kk, v, ex, decay, level_is, row_bit, st_ref, slot, *, d):
    c = GLA_C
    v = v.astype(BF16)
    att = jnp.where(level_is[-1], _dot_nt(q.astype(BF16), kk.astype(BF16)), 0.0)
    for lv in range(GLA_LEVELS):
        ex_l = ex[lv * c:(lv + 1) * c]
        half = 1 << lv
        if half < SUBLANES:
            is_q = row_bit[lv] if d == 0 else jnp.logical_not(row_bit[lv])
            scaled = jnp.where(is_q, q, kk) * ex_l
            ql = jnp.where(is_q, scaled, 0.0)
            kl = jnp.where(is_q, 0.0, scaled)
            att = jnp.where(level_is[lv], _dot_nt(ql.astype(BF16), kl.astype(BF16)), att)
        else:
            zeros = jnp.zeros((half, DH), F32)
            blocks = [slice(blk * half, (blk + 1) * half) for blk in range(c // half)]
            q_side = [(blk & 1) == 1 - d for blk in range(c // half)]
            ql = jnp.concatenate([q[rows] * ex_l[rows] for rows, qs in zip(blocks, q_side) if qs], axis=0)
            kl = jnp.concatenate([zeros if qs else kk[rows] * ex_l[rows] for rows, qs in zip(blocks, q_side)], axis=0)
            prod = _dot_nt(ql.astype(BF16), kl.astype(BF16))
            parts, used = [], 0
            for rows, qs in zip(blocks, q_side):
                if qs:
                    parts.append(jnp.where(level_is[lv][rows], prod[used:used + half], att[rows]))
                    used += half
                else:
                    parts.append(att[rows])
            att = jnp.concatenate(parts, axis=0)

    q_in = (q * ex[GLA_LEVELS * c:(GLA_LEVELS + 1) * c]).astype(BF16)
    k_out = (kk * ex[(GLA_LEVELS + 1) * c:]).astype(BF16)
    st = st_ref[slot]
    o = _dot(att.astype(BF16), v) + _dot_nt(q_in, st.astype(BF16))
    st_ref[slot] = st * decay + _dot_tn(v, k_out)
    return o


def _gla_body(qf_ref, ff_ref, vf_ref, qb_ref, fb_ref, vb_ref, lb_ref, msk_ref, lvl_ref, of_ref, ob_ref, st_ref,
              *, layer):
    c = GLA_C

    @pl.when(pl.program_id(2) == 0)
    def _():
        st_ref[...] = jnp.zeros_like(st_ref)

    lvl = lvl_ref[...]
    level_is = [lvl == lv for lv in range(GLA_LEVELS)] + [lvl == -1]
    row = lax.broadcasted_iota(jnp.int32, (c, DH), 0)
    row_bit = [((row >> lv) & 1) == 1 for lv in range(GLA_LEVELS)]
    for d, (q_ref, f_ref, v_ref, o_ref) in enumerate(((qf_ref, ff_ref, vf_ref, of_ref),
                                                      (qb_ref, fb_ref, vb_ref, ob_ref))):
        g, kk = _gla_gates(f_ref[...], lb_ref[d], layer)
        g_hi = g.astype(BF16)
        g_lo = (g - g_hi.astype(F32)).astype(BF16)
        ex = jnp.exp(_dot(msk_ref[d], jnp.concatenate([g_hi, g_lo], axis=0)))
        decay = jnp.exp(jnp.sum(g, axis=0, keepdims=True))
        for hh in range(GLA_HB):
            cols = slice(hh * DH, (hh + 1) * DH)
            o_ref[:, cols] = _gla_chain(q_ref[:, cols], kk[:, cols], v_ref[:, cols], ex[:, cols], decay[:, cols],
                                        level_is, row_bit, st_ref, d * GLA_HB + hh, d=d)


def gla(p, lb_raw, masks, lvl, b, t, tc, layer):
    c = GLA_C
    m = p.shape[0]
    n_ctx_chunks = tc // c
    n_lat_chunks = t // c
    n_chunks = n_ctx_chunks + n_lat_chunks
    ctx0 = (b * t) // c
    n_even = lb_raw.shape[1]
    wb = GLA_HB * DH

    def blk(bi, i, d):
        jc = i if d == 0 else n_ctx_chunks - 1 - i
        jl = i - n_ctx_chunks if d == 0 else n_chunks - 1 - i
        return jnp.where(i < n_ctx_chunks, ctx0 + bi * n_ctx_chunks + jc, bi * n_lat_chunks + jl)

    def spec(section, d):
        col0 = (P_FRONT + section * H * DH) // wb
        return pl.BlockSpec((c, wb), lambda bi, hh, i: (blk(bi, i, d), col0 + hh))

    out_spec = lambda d: pl.BlockSpec((c, wb), lambda bi, hh, i: (blk(bi, i, d), hh))
    return pl.pallas_call(
        functools.partial(_gla_body, layer=layer),
        grid=(b, H // GLA_HB, n_chunks),
        in_specs=[spec(0, 0), spec(1, 0), spec(3, 0), spec(0, 1), spec(2, 1), spec(3, 1),
                  pl.BlockSpec((2, n_even, wb), lambda bi, hh, i: (0, 0, hh)),
                  pl.BlockSpec((2, (GLA_LEVELS + 2) * c, 2 * c), lambda bi, hh, i: (0, 0, 0)),
                  pl.BlockSpec((c, c), lambda bi, hh, i: (0, 0))],
        out_specs=[out_spec(0), out_spec(1)],
        out_shape=[jax.ShapeDtypeStruct((m, H * DH), F32)] * 2,
        scratch_shapes=[pltpu.VMEM((2 * GLA_HB, DH, DH), F32)],
        compiler_params=_cp("parallel", "parallel", "arbitrary"),
        name="hgrn_scan",
    )(p, p, p, p, p, p, lb_raw, masks, lvl)


def _hg_merge_body(of_ref, ob_ref, g_ref, n_ref, y_ref):
    o = of_ref[...] + ob_ref[...]
    gate = g_ref[...]
    gain = n_ref[...]
    for h in range(H):
        sl = slice(h * DH, (h + 1) * DH)
        gh = gate[:, sl]
        y_ref[:, sl] = (_rms(o[:, sl]) * gain * (gh * _sigmoid(gh))).astype(BF16)


def hg_merge(o_f, o_b, p, hg_norm):
    m = p.shape[0]
    tm = 512
    w = H * DH
    row = lambda i: (i, 0)
    return pl.pallas_call(
        _hg_merge_body,
        grid=(m // tm,),
        in_specs=[pl.BlockSpec((tm, w), row),
                  pl.BlockSpec((tm, w), row),
                  pl.BlockSpec((tm, w), lambda i: (i, (P_WIDTH - w) // w)),
                  pl.BlockSpec((1, DH), lambda i: (0, 0))],
        out_specs=pl.BlockSpec((tm, w), row),
        out_shape=jax.ShapeDtypeStruct((m, w), BF16),
        compiler_params=_cp("parallel"),
        name="hgrn_merge",
    )(o_f, o_b, p, hg_norm.reshape(1, DH))


RES_TM = 512
RES_TN = 512
RES_ROWS = 128


def _mm_res_body(*refs, groups, gate, shift, scale, nj, has_bias, has_next):
    refs = list(refs)
    a_refs = [[refs.pop(0) for _ in pieces] for _, pieces in groups]
    w_ref = refs.pop(0)
    b_ref = refs.pop(0) if has_bias else None
    x_ref, gp_ref, mc_ref = refs.pop(0), refs.pop(0), refs.pop(0)
    gn_ref, mn_ref = (refs.pop(0), refs.pop(0)) if has_next else (None, None)
    xo_ref = refs.pop(0)
    uo_ref = refs.pop(0) if has_next else None
    i = pl.program_id(0)
    j = pl.program_id(1)
    tn = w_ref.shape[1]
    cols = pl.ds(pl.multiple_of(j * tn, tn), tn)

    w = w_ref[...]
    koff = 0
    for gi, ((kw, pieces), piece_refs) in enumerate(zip(groups, a_refs)):
        wk = w[koff:koff + kw]
        koff += kw
        for a_ref, (r0, nr) in zip(piece_refs, pieces):
            def project(a_ref=a_ref, wk=wk, gi=gi):
                part = _dot(a_ref[...], wk)
                if gi == 0:
                    xo_ref[:, cols] = part
                else:
                    xo_ref[:, cols] += part

            if len(pieces) == 1:
                project()
            else:
                pl.when((i >= r0) & (i < r0 + nr))(project)

    @pl.when(j == nj - 1)
    def _():
        mc = mc_ref[0]
        gate_gain = mc[:, gate * D:(gate + 1) * D] * gp_ref[...]
        if has_next:
            mn = mn_ref[0]
            scale_gain = gn_ref[...] * (1.0 + mn[:, scale * D:(scale + 1) * D])
            shift_v = mn[:, shift * D:(shift + 1) * D]

        def rows_pass(r, carry):
            rows = pl.ds(pl.multiple_of(r * RES_ROWS, RES_ROWS), RES_ROWS)
            y = xo_ref[rows, :]
            if has_bias:
                y = y + b_ref[...]
            xn = x_ref[rows, :] + _rms(y) * gate_gain
            xo_ref[rows, :] = xn
            if has_next:
                uo_ref[rows, :] = (_rms(xn) * scale_gain + shift_v).astype(BF16)
            return carry

        lax.fori_loop(0, xo_ref.shape[0] // RES_ROWS, rows_pass, 0)


def matmul_residual(groups, w, layer, bias, x, m, g_post, mods_cur, gate, n_lat, seq, tm, nxt=None):
    tn = RES_TN
    nj = D // tn
    kdim = w.shape[1]
    mrow = lambda i, j: (_mod_row(i, tm, n_lat, seq), 0, 0)
    row = lambda i, j: (i, 0)
    fixed = lambda i, j: (0, 0)

    def piece_spec(arr, r0, nr):
        return pl.BlockSpec((tm, arr.shape[1]), lambda i, j: (jnp.clip(i - r0, 0, nr - 1), 0))

    in_specs = [piece_spec(*pc) for g in groups for pc in g]
    in_specs.append(pl.BlockSpec((None, kdim, tn), lambda i, j: (layer, 0, j)))
    args = [pc[0] for g in groups for pc in g] + [w]
    if bias is not None:
        in_specs.append(pl.BlockSpec((1, D), fixed))
        args.append(bias.reshape(1, D))
    in_specs += [pl.BlockSpec((tm, D), row),
                 pl.BlockSpec((1, D), fixed),
                 pl.BlockSpec((1, 1, 6 * D), mrow)]
    args += [x, g_post.reshape(1, D), mods_cur]
    out_specs = [pl.BlockSpec((tm, D), row)]
    out_shape = [jax.ShapeDtypeStruct((m, D), F32)]
    shift = scale = 0
    if nxt is not None:
        g_next, mods_next, shift, scale = nxt
        in_specs += [pl.BlockSpec((1, D), fixed), pl.BlockSpec((1, 1, 6 * D), mrow)]
        args += [g_next.reshape(1, D), mods_next]
        out_specs.append(pl.BlockSpec((tm, D), row))
        out_shape.append(jax.ShapeDtypeStruct((m, D), BF16))
    group_meta = tuple((g[0][0].shape[1], tuple(pc[1:] for pc in g)) for g in groups)
    assert sum(kw for kw, _ in group_meta) == kdim
    outs = pl.pallas_call(
        functools.partial(_mm_res_body, groups=group_meta, gate=gate, shift=shift, scale=scale,
                          nj=nj, has_bias=bias is not None, has_next=nxt is not None),
        grid=(m // tm, nj),
        in_specs=in_specs,
        out_specs=out_specs,
        out_shape=out_shape,
        compiler_params=_cp("parallel", "arbitrary"),
        name="proj_residual",
    )(*args)
    return (outs[0], outs[1]) if nxt is not None else (outs[0], None)


GATED_HALO = 16


def _mm_gated_body(*refs, conv, silu, lat_tiles, seq_lat, seq_ctx):
    if conv:
        x_ref, xp_ref, xn_ref, wa_ref, wg_ref, ca_ref, cg_ref, ba_ref, bg_ref, o_ref = refs
    else:
        x_ref, wa_ref, wg_ref, ba_ref, bg_ref, o_ref = refs
    x = x_ref[...]
    wa = wa_ref[...].astype(BF16)
    wg = wg_ref[...].astype(BF16)
    za = _dot(x, wa)
    zg = _dot(x, wg)
    if conv:
        i = pl.program_id(0)
        tm, tn = za.shape
        halo = jnp.concatenate([xp_ref[...], xn_ref[...]], axis=0)
        ha = _dot(halo, wa)
        hg = _dot(halo, wg)
        is_ctx = i >= lat_tiles
        seq_m1 = jnp.where(is_ctx, seq_ctx - 1, seq_lat - 1)
        starts_seq = ((i * tm) & seq_m1) == 0
        ends_seq = (((i + 1) * tm) & seq_m1) == 0
        sub = lax.broadcasted_iota(jnp.int32, (SUBLANES, tn), 0)
        top = sub == 0
        bottom = sub == SUBLANES - 1
        inner = range(seq_ctx, tm, seq_ctx)
        g = SUBLANES

        def conv3(z, h, cw):
            up = pltpu.roll(z, 1, 0)
            dn = pltpu.roll(z, tm - 1, 0)
            before = jnp.where(starts_seq, 0.0, h[GATED_HALO - 1:GATED_HALO])
            after = jnp.where(ends_seq, 0.0, h[GATED_HALO:GATED_HALO + 1])
            up_parts, dn_parts = [jnp.where(top, before, up[:g])], []
            up_lo, dn_lo = g, 0
            for r in inner:
                up_parts += [up[up_lo:r], jnp.where(jnp.logical_and(top, is_ctx), 0.0, up[r:r + g])]
                dn_parts += [dn[dn_lo:r - g], jnp.where(jnp.logical_and(bottom, is_ctx), 0.0, dn[r - g:r])]
                up_lo, dn_lo = r + g, r
            up_parts.append(up[up_lo:])
            dn_parts += [dn[dn_lo:tm - g], jnp.where(bottom, after, dn[tm - g:])]
            up = jnp.concatenate(up_parts, axis=0)
            dn = jnp.concatenate(dn_parts, axis=0)
            return cw[0:1] * up + cw[1:2] * z + cw[2:3] * dn

        za = conv3(za, ha, ca_ref[...])
        zg = conv3(zg, hg, cg_ref[...])
    za = za + ba_ref[...]
    zg = zg + bg_ref[...]
    sg = _sigmoid(zg)
    o_ref[...] = (za * (zg * sg if silu else sg)).astype(o_ref.dtype)


def matmul_gated(x, w, layer, conv_w, bias, *, m, tm, tn, n_lat, seq_lat, seq_ctx, silu, out_dtype, name):
    kdim = x.shape[1]
    n = w.shape[2] // 2
    nj = n // tn
    first = lambda i, j: (layer, 0, j)
    second = lambda i, j: (layer, 0, j + nj)
    bias = bias.reshape(bias.shape[0], 1, 2 * n)
    in_specs = [pl.BlockSpec((tm, kdim), lambda i, j: (i, 0))]
    args = [x]
    if conv_w is not None:
        hb = tm // GATED_HALO
        last_hb = x.shape[0] // GATED_HALO - 1
        in_specs += [pl.BlockSpec((GATED_HALO, kdim), lambda i, j: (jnp.maximum(i * hb - 1, 0), 0)),
                     pl.BlockSpec((GATED_HALO, kdim), lambda i, j: (jnp.minimum((i + 1) * hb, last_hb), 0))]
        args += [x, x]
    in_specs += [pl.BlockSpec((None, kdim, tn), first), pl.BlockSpec((None, kdim, tn), second)]
    args += [w, w]
    if conv_w is not None:
        in_specs += [pl.BlockSpec((None, conv_w.shape[1], tn), first),
                     pl.BlockSpec((None, conv_w.shape[1], tn), second)]
        args += [conv_w, conv_w]
    in_specs += [pl.BlockSpec((None, 1, tn), first), pl.BlockSpec((None, 1, tn), second)]
    args += [bias, bias]
    return pl.pallas_call(
        functools.partial(_mm_gated_body, conv=conv_w is not None, silu=silu, lat_tiles=n_lat // tm,
                          seq_lat=seq_lat, seq_ctx=seq_ctx),
        grid=(m // tm, nj),
        in_specs=in_specs,
        out_specs=pl.BlockSpec((tm, tn), lambda i, j: (i, j)),
        out_shape=jax.ShapeDtypeStruct((m, n), out_dtype),
        compiler_params=_cp("parallel", "parallel"),
        name=name,
    )(*args)


def _dwconv_ln_body(prev_ref, cur_ref, next_ref, w_ref, b_ref, g_ref, beta_ref, o_ref, ext_ref, sh_ref, acc_ref,
                    *, tiles_per_seq, n_lat_tiles):
    i = pl.program_id(0)
    tm = cur_ref.shape[0]
    hw = CONV_HALO
    is_lat = i < n_lat_tiles
    j = i % tiles_per_seq
    has_prev = jnp.logical_and(is_lat, j > 0).astype(F32)
    has_next = jnp.logical_and(is_lat, j < tiles_per_seq - 1).astype(F32)
    ext_ref[0:hw, :] = prev_ref[tm - hw:, :] * has_prev
    ext_ref[hw:hw + tm, :] = cur_ref[...]
    ext_ref[hw + tm:, :] = next_ref[0:hw, :] * has_next

    sh_rows = sh_ref.shape[1]

    def lane_block(cb, carry):
        col = pl.multiple_of(cb * DH, DH)
        wv = w_ref[:, pl.ds(col, DH)]
        for s in range(SUBLANES):
            sh_ref[s] = ext_ref[s:s + sh_rows, pl.ds(col, DH)]
        for r0 in range(0, tm, 128):
            acc = jnp.zeros((128, DH), F32)
            for k in range(CONV_W):
                off = r0 + k + hw - (CONV_W - 1) // 2
                s = off % SUBLANES
                acc = acc + sh_ref[s, off - s:off - s + 128, :] * wv[k:k + 1]
            acc_ref[r0:r0 + 128, pl.ds(col, DH)] = acc
        return carry

    lax.fori_loop(0, D // DH, lane_block, 0)
    hcv = acc_ref[...] + b_ref[...]
    mu = jnp.mean(hcv, axis=-1, keepdims=True)
    xc = hcv - mu
    var = jnp.mean(xc * xc, axis=-1, keepdims=True)
    y = xc * lax.rsqrt(var + EPS) * g_ref[...] + beta_ref[...]
    o_ref[...] = (y * _sigmoid(y)).astype(BF16)


def dwconv_ln(hx, w_dw, b_dw, ln_g, ln_b, n_lat, seq):
    m = hx.shape[0]
    tm = 256
    nt = m // tm
    fixed = lambda i: (0, 0)
    vec = lambda a: a.reshape(1, D)
    return pl.pallas_call(
        functools.partial(_dwconv_ln_body, tiles_per_seq=seq // tm, n_lat_tiles=n_lat // tm),
        grid=(nt,),
        in_specs=[pl.BlockSpec((tm, D), lambda i: (jnp.maximum(i - 1, 0), 0)),
                  pl.BlockSpec((tm, D), lambda i: (i, 0)),
                  pl.BlockSpec((tm, D), lambda i: (jnp.minimum(i + 1, nt - 1), 0)),
                  pl.BlockSpec((CONV_W, D), fixed),
                  pl.BlockSpec((1, D), fixed),
                  pl.BlockSpec((1, D), fixed),
                  pl.BlockSpec((1, D), fixed)],
        out_specs=pl.BlockSpec((tm, D), lambda i: (i, 0)),
        out_shape=jax.ShapeDtypeStruct((m, D), BF16),
        scratch_shapes=[pltpu.VMEM((tm + 2 * CONV_HALO, D), F32),
                        pltpu.VMEM((SUBLANES, tm + 2 * CONV_HALO - SUBLANES, DH), F32),
                        pltpu.VMEM((tm, D), F32)],
        compiler_params=_cp("parallel"),
        name="dwconv_ln",
    )(hx, hx, hx, w_dw, vec(b_dw), vec(ln_g), vec(ln_b))


def _rope_tables(b, t, tc):
    rows = t // GRID_W
    row = jnp.repeat(jnp.arange(rows), GRID_W).astype(F32)
    col = jnp.tile(jnp.arange(GRID_W), rows).astype(F32)
    pairs = ROPE // 4
    freqs = ROPE_BASE ** (-jnp.arange(pairs, dtype=F32) / pairs)
    ar = row[:, None] * freqs
    ac = col[:, None] * freqs
    cos = jnp.concatenate([jnp.cos(ar), jnp.cos(ar), jnp.cos(ac), jnp.cos(ac), jnp.ones((t, DH - ROPE), F32)], axis=1)
    sin = jnp.concatenate([-jnp.sin(ar), jnp.sin(ar), -jnp.sin(ac), jnp.sin(ac), jnp.zeros((t, DH - ROPE), F32)], axis=1)
    cos = jnp.concatenate([jnp.tile(cos, (b, 1)), jnp.ones((b * tc, DH), F32)], axis=0)
    sin = jnp.concatenate([jnp.tile(sin, (b, 1)), jnp.zeros((b * tc, DH), F32)], axis=0)
    return cos, sin


def _pick(n, *cands):
    for cand in cands:
        if n % cand == 0:
            return cand
    raise ValueError(f"no tile for {n}")


def kernel(x, c, ctx, c_ctx, w_mod, b_mod, norm_gains, w_in_ab, mla_q_norm, w_q_up, mla_kv_norm, w_kv_up, hgrn_lb, hgrn_norm, w_out_ab, conv_w_pw1, conv_b_pw1, conv_w_dw, conv_b_dw, conv_ln_g, conv_ln_b, conv_w_pw2, conv_b_pw2, ffn_w_up, ffn_w_conv, ffn_b_conv, ffn_w_down):
    b, t, _ = x.shape
    tc = ctx.shape[1]
    depth = w_mod.shape[0]
    n_lat = b * t
    n_ctx = b * tc
    assert x.shape[2] == D and depth % 2 == 0 and b < 8
    assert t % 512 == 0 and tc % 256 == 0 and n_ctx % 512 == 0 and t % n_ctx == 0
    assert t & (t - 1) == 0 and tc & (tc - 1) == 0
    tg = _pick(n_ctx, 1024, 512)
    tr = RES_TM
    lat_tiles = n_lat // tr
    ctx_tiles = n_ctx // tr
    w_out_bf, w_pw2_bf, w_down_bf = w_out_ab.astype(BF16), conv_w_pw2.astype(BF16), ffn_w_down.astype(BF16)
    w_in_p = repack_in_proj(w_in_ab)

    hcur = jnp.concatenate([x.reshape(n_lat, D), ctx.reshape(n_ctx, D)], axis=0)
    cc = jnp.concatenate([c, c_ctx[None, :], jnp.zeros((7 - b, D), F32)], axis=0)
    mods = modulation(cc, w_mod, b_mod).reshape(depth, 8, 1, 6 * D)
    cos, sin = _rope_tables(b, t, tc)
    masks_np, lvl_np = _gla_tables()
    masks = jnp.asarray(masks_np, BF16)
    lvl = jnp.asarray(lvl_np)

    u = norm_mod(hcur, norm_gains[0, 0], mods[0], n_lat, t, shift=0, scale=1)
    for l in range(depth):
        last = l == depth - 1
        j = l // 2
        m = n_lat if last else n_lat + n_ctx
        row_tiles = m // tr
        g_pre1, g_post1, g_pre2, g_post2 = norm_gains[l]
        if l % 2 == 0:
            wq =jnp.pad(w_q_up[j].reshape(QL, H, DH + ROPE), ((0, 0), (0, 0), (0, DH - ROPE)))
            wq = wq.reshape(QL, 2 * H * DH).astype(BF16)
            wkv = w_kv_up[j].reshape(KVL, H, 2 * DH)
            wkv = jnp.concatenate([wkv[:, :, :DH].reshape(KVL, H * DH), wkv[:, :, DH:].reshape(KVL, H * DH)], axis=1)
            p = matmul(u, w_in_p, j, _pick(m, 1024, 512), 512)
            q, k, v = mla_prep(p, mla_q_norm[j], wq, mla_kv_norm[j], wkv.astype(BF16), cos, sin)
            att_lat = attention_latent(q, k, v, b, t, tc)
            att_ctx = attention_context(q, k, v, b, t, tc)
            o_f, o_b = gla(p, hgrn_lb, masks, lvl, b, t, tc, j)
            hg = hg_merge(o_f, o_b, p, hgrn_norm[j])
            att_pieces = [(att_lat, 0, lat_tiles)] + ([] if last else [(att_ctx, lat_tiles, ctx_tiles)])
            groups = [att_pieces, [(hg, 0, row_tiles)]]
            w_o, b_o = w_out_bf, None
        else:
            glu = matmul_gated(u, conv_w_pw1, j, None, conv_b_pw1, m=m, tm=tg, tn=512, n_lat=n_lat, seq_lat=t,
                               seq_ctx=tc, silu=False, out_dtype=F32, name="conformer_glu")
            mix = dwconv_ln(glu, conv_w_dw[j], conv_b_dw[j], conv_ln_g[j], conv_ln_b[j], n_lat, t)
            groups = [[(mix, 0, row_tiles)]]
            w_o, b_o = w_pw2_bf, conv_b_pw2[j]
        hcur, u = matmul_residual(groups, w_o, j, b_o, hcur, m, g_post1, mods[l], 2, n_lat, t, tr,
                                  nxt=(g_pre2, mods[l], 3, 4))
        act = matmul_gated(u, ffn_w_up, l, ffn_w_conv, ffn_b_conv, m=m, tm=tg, tn=512, n_lat=n_lat, seq_lat=t,
                           seq_ctx=tc, silu=True, out_dtype=BF16, name="ffn_up")
        nxt = None if last else (norm_gains[l + 1, 0], mods[l + 1], 0, 1)
        hcur, u = matmul_residual([[(act, 0, row_tiles)]], w_down_bf, l, None, hcur, m, g_post2, mods[l], 5,
                                  n_lat, t, tr, nxt=nxt)
    return hcur.reshape(b, t, D)
```

```python
import functools

import numpy as np
import jax
import jax.numpy as jnp
from jax import lax
from jax.experimental import pallas as pl
from jax.experimental.pallas import tpu as pltpu

F32 = jnp.float32
BF16 = jnp.bfloat16

D = 2048
H = D // 256
DH = 128
ROPE = 64
QL = D // 4
KVL = D // 8
DFF = (11 * D) // 4
GRID_W = 64
ROPE_BASE = 10000.0
CONV_W = 31
CONV_HALO = 16
EPS = 1e-6
ATTN_SCALE = (DH + ROPE) ** -0.5
LOG2_E = float(np.log2(np.e))
Q_SCALE = ATTN_SCALE * LOG2_E
ATTN_HB = 4
P_FRONT = 1024
P_WIDTH = P_FRONT + 5 * H * DH
GLA_C = 128
GLA_LEVELS = 7
GLA_HB = 8
SUBLANES = 8
VMEM_LIMIT = 56 * 1024 * 1024


def _cp(*sem):
    return pltpu.CompilerParams(dimension_semantics=sem, vmem_limit_bytes=VMEM_LIMIT)


def _sigmoid(x):
    return 1.0 / (1.0 + jnp.exp(-x))


def _rms(x):
    return x * lax.rsqrt(jnp.mean(x * x, axis=-1, keepdims=True) + EPS)


def _dot(a, b):
    return jnp.dot(a, b, preferred_element_type=F32)


def _dot_nt(a, b):
    return lax.dot_general(a, b, (((1,), (1,)), ((), ())), preferred_element_type=F32)


def _dot_tn(a, b):
    return lax.dot_general(a, b, (((0,), (0,)), ((), ())), preferred_element_type=F32)


def _mod_row(i, tm, n_lat, seq):
    return jnp.where(i < n_lat // tm, i // (seq // tm), n_lat // seq)


def _mod_body(c_ref, w_ref, b_ref, o_ref):
    c = c_ref[...]
    s = (c * _sigmoid(c)).astype(BF16)
    o_ref[0] = _dot(s, w_ref[0].astype(BF16)) + b_ref[0]


def modulation(cc, w_mod, b_mod):
    depth, _, n = w_mod.shape
    rows = cc.shape[0]
    tn = 1024
    return pl.pallas_call(
        _mod_body,
        grid=(depth, n // tn),
        in_specs=[pl.BlockSpec((rows, D), lambda l, j: (0, 0)),
                  pl.BlockSpec((1, D, tn), lambda l, j: (l, 0, j)),
                  pl.BlockSpec((1, 1, tn), lambda l, j: (l, 0, j))],
        out_specs=pl.BlockSpec((1, rows, tn), lambda l, j: (l, 0, j)),
        out_shape=jax.ShapeDtypeStruct((depth, rows, n), F32),
        compiler_params=_cp("parallel", "parallel"),
        name="modulation",
    )(cc, w_mod, b_mod.reshape(depth, 1, n))


def _norm_mod_body(x_ref, g_ref, m_ref, o_ref, *, shift, scale):
    m = m_ref[0]
    y = _rms(x_ref[...]) * g_ref[...]
    o_ref[...] = (y * (1.0 + m[:, scale * D:(scale + 1) * D]) + m[:, shift * D:(shift + 1) * D]).astype(BF16)


def norm_mod(x, gain, mods, n_lat, seq, shift, scale):
    m = x.shape[0]
    tm = 256
    return pl.pallas_call(
        functools.partial(_norm_mod_body, shift=shift, scale=scale),
        grid=(m // tm,),
        in_specs=[pl.BlockSpec((tm, D), lambda i: (i, 0)),
                  pl.BlockSpec((1, D), lambda i: (0, 0)),
                  pl.BlockSpec((1, 1, 6 * D), lambda i: (_mod_row(i, tm, n_lat, seq), 0, 0))],
        out_specs=pl.BlockSpec((tm, D), lambda i: (i, 0)),
        out_shape=jax.ShapeDtypeStruct((m, D), BF16),
        compiler_params=_cp("parallel"),
        name="norm_mod",
    )(x, gain.reshape(1, D), mods)


def _repack_in_body(w_ref, o_ref):
    front = QL + KVL + ROPE
    w = w_ref[...]
    o_ref[:, :front] = w[:, :front].astype(BF16)
    o_ref[:, front:P_FRONT] = jnp.zeros((w.shape[0], P_FRONT - front), BF16)
    o_ref[:, P_FRONT:] = w[:, front:].astype(BF16)


def repack_in_proj(w_in):
    layers, k, n = w_in.shape
    tr = 256
    return pl.pallas_call(
        _repack_in_body,
        grid=(layers, k // tr),
        in_specs=[pl.BlockSpec((None, tr, n), lambda l, i: (l, i, 0))],
        out_specs=pl.BlockSpec((None, tr, P_WIDTH), lambda l, i: (l, i, 0)),
        out_shape=jax.ShapeDtypeStruct((layers, k, P_WIDTH), BF16),
        compiler_params=_cp("parallel", "parallel"),
        name="repack_in_proj",
    )(w_in)


def _mm_body(x_ref, w_ref, o_ref):
    o_ref[...] = _dot(x_ref[...], w_ref[...])


def matmul(x, w, layer, tm, tn):
    m, k = x.shape
    n = w.shape[2]
    return pl.pallas_call(
        _mm_body,
        grid=(m // tm, n // tn),
        in_specs=[pl.BlockSpec((tm, k), lambda i, j: (i, 0)),
                  pl.BlockSpec((None, k, tn), lambda i, j: (layer, 0, j))],
        out_specs=pl.BlockSpec((tm, tn), lambda i, j: (i, j)),
        out_shape=jax.ShapeDtypeStruct((m, n), F32),
        compiler_params=_cp("parallel", "parallel"),
        name="in_proj",
    )(x, w)


def _mla_prep_body(p_ref, qn_ref, wq_ref, kvn_ref, wkv_ref, cos_ref, sin_ref, q_ref, k_ref, v_ref):
    p = p_ref[...]
    tm = p.shape[0]
    cos = cos_ref[...]
    sin = sin_ref[...]
    lane = lax.broadcasted_iota(jnp.int32, (tm, DH), 1)
    first_half = (lane & 31) < 16

    def rope(x):
        partner = jnp.where(first_half, pltpu.roll(x, DH - 16, 1), pltpu.roll(x, 16, 1))
        return x * cos + partner * sin

    cq = (_rms(p[:, :QL]) * qn_ref[...]).astype(BF16)
    ckv = (_rms(p[:, QL:QL + KVL]) * kvn_ref[...]).astype(BF16)
    kr = rope(p[:, QL + KVL:QL + KVL + DH]).astype(BF16)
    q = _dot(cq, wq_ref[...])
    kv = _dot(ckv, wkv_ref[...])
    for h in range(H):
        lo = 2 * DH * h
        q_ref[:, lo:lo + DH] = (q[:, lo:lo + DH] * Q_SCALE).astype(BF16)
        q_ref[:, lo + DH:lo + 2 * DH] = (rope(q[:, lo + DH:lo + 2 * DH]) * Q_SCALE).astype(BF16)
        k_ref[:, lo:lo + DH] = kv[:, DH * h:DH * (h + 1)].astype(BF16)
        k_ref[:, lo + DH:lo + 2 * DH] = kr
    v_ref[...] = kv[:, H * DH:].astype(BF16)


def mla_prep(p, q_norm, wq, kv_norm, wkv, cos, sin):
    m = p.shape[0]
    tm = 512
    row = lambda i: (i, 0)
    fixed = lambda i: (0, 0)
    return pl.pallas_call(
        _mla_prep_body,
        grid=(m // tm,),
        in_specs=[pl.BlockSpec((tm, P_FRONT), row),
                  pl.BlockSpec((1, QL), fixed),
                  pl.BlockSpec((QL, 2 * H * DH), fixed),
                  pl.BlockSpec((1, KVL), fixed),
                  pl.BlockSpec((KVL, 2 * H * DH), fixed),
                  pl.BlockSpec((tm, DH), row),
                  pl.BlockSpec((tm, DH), row)],
        out_specs=[pl.BlockSpec((tm, 2 * H * DH), row),
                   pl.BlockSpec((tm, 2 * H * DH), row),
                   pl.BlockSpec((tm, H * DH), row)],
        out_shape=[jax.ShapeDtypeStruct((m, 2 * H * DH), BF16),
                   jax.ShapeDtypeStruct((m, 2 * H * DH), BF16),
                   jax.ShapeDtypeStruct((m, H * DH), BF16)],
        compiler_params=_cp("parallel"),
        name="mla_prep",
    )(p, q_norm.reshape(1, QL), wq, kv_norm.reshape(1, KVL), wkv, cos, sin)


def _attn_lat_body(q_ref, kl_ref, vl_ref, kc_ref, vc_ref, o_ref):
    for hh in range(ATTN_HB):
        qk = slice(hh * 2 * DH, (hh + 1) * 2 * DH)
        vo = slice(hh * DH, (hh + 1) * DH)
        q = q_ref[:, qk]
        sl = _dot_nt(q, kl_ref[:, qk])
        sc = _dot_nt(q, kc_ref[:, qk])
        mx = jnp.maximum(jnp.max(sl, axis=-1, keepdims=True), jnp.max(sc, axis=-1, keepdims=True))
        el = jnp.exp2(sl - mx)
        ec = jnp.exp2(sc - mx)
        den = jnp.sum(el, axis=-1, keepdims=True) + jnp.sum(ec, axis=-1, keepdims=True)
        o = _dot(el.astype(BF16), vl_ref[:, vo]) + _dot(ec.astype(BF16), vc_ref[:, vo])
        o_ref[:, vo] = (o / den).astype(BF16)


def _attn_ctx_body(q_ref, kc_ref, vc_ref, o_ref):
    sc = _dot_nt(q_ref[...], kc_ref[...])
    ec = jnp.exp2(sc - jnp.max(sc, axis=-1, keepdims=True))
    o = _dot(ec.astype(BF16), vc_ref[...])
    o_ref[...] = (o / jnp.sum(ec, axis=-1, keepdims=True)).astype(BF16)


def attention_latent(q, k, v, b, t, tc):
    n_lat = b * t
    tq = 512
    nq = t // tq
    cb = n_lat // tc
    wqk = ATTN_HB * 2 * DH
    wv = ATTN_HB * DH
    return pl.pallas_call(
        _attn_lat_body,
        grid=(b, H // ATTN_HB, nq),
        in_specs=[pl.BlockSpec((tq, wqk), lambda bi, h, qi: (bi * nq + qi, h)),
                  pl.BlockSpec((t, wqk), lambda bi, h, qi: (bi, h)),
                  pl.BlockSpec((t, wv), lambda bi, h, qi: (bi, h)),
                  pl.BlockSpec((tc, wqk), lambda bi, h, qi: (cb + bi, h)),
                  pl.BlockSpec((tc, wv), lambda bi, h, qi: (cb + bi, h))],
        out_specs=pl.BlockSpec((tq, wv), lambda bi, h, qi: (bi * nq + qi, h)),
        out_shape=jax.ShapeDtypeStruct((n_lat, H * DH), BF16),
        compiler_params=_cp("parallel", "parallel", "parallel"),
        name="attn_latent",
    )(q, k, v, k, v)


def attention_context(q, k, v, b, t, tc):
    cb = (b * t) // tc
    return pl.pallas_call(
        _attn_ctx_body,
        grid=(b, H),
        in_specs=[pl.BlockSpec((tc, 2 * DH), lambda bi, h: (cb + bi, h)),
                  pl.BlockSpec((tc, 2 * DH), lambda bi, h: (cb + bi, h)),
                  pl.BlockSpec((tc, DH), lambda bi, h: (cb + bi, h))],
        out_specs=pl.BlockSpec((tc, DH), lambda bi, h: (bi, h)),
        out_shape=jax.ShapeDtypeStruct((b * tc, H * DH), BF16),
        compiler_params=_cp("parallel", "parallel"),
        name="attn_context",
    )(q, k, v)


def _gla_tables():
    c = GLA_C
    t = np.arange(c)[:, None]
    r = np.arange(c)[None, :]
    mats = []
    for lv in range(GLA_LEVELS):
        same = (t >> (lv + 1)) == (r >> (lv + 1))
        bt = (t >> lv) & 1
        br = (r >> lv) & 1
        upper = same & (bt == 1) & (br == 1) & (r <= t)
        lower = same & (bt == 0) & (br == 0) & (r > t)
        mats.append(upper | lower)
    mats.append(r <= t)
    mats.append(r > t)
    fwd = np.stack(mats).astype(np.float32)
    bwd = fwd[:, ::-1, ::-1]
    masks = np.stack([fwd, bwd]).reshape(2, (GLA_LEVELS + 2) * c, c)
    masks = np.concatenate([masks, masks], axis=2)
    x = t ^ r
    lvl = np.where(x == 0, -1, np.floor(np.log2(np.maximum(x, 1)))).astype(np.int32)
    return masks, lvl


def _gla_gates(x, lb_raw, layer):
    log_sig = jnp.minimum(x, 0.0) - jnp.log1p(jnp.exp(-jnp.abs(x)))
    if layer == 0:
        g = log_sig
        kk = 1.0 / (1.0 + jnp.exp(x))
    else:
        e = jnp.exp(lb_raw - jnp.max(lb_raw, axis=0, keepdims=True))
        prob = e / jnp.sum(e, axis=0, keepdims=True)
        lb = jnp.sum(prob[1:layer + 1], axis=0, keepdims=True)
        a = jnp.log(lb)
        bb = jnp.log1p(-lb) + log_sig
        g = jnp.maximum(a, bb) + jnp.log1p(jnp.exp(-jnp.abs(a - bb)))
        kk = (1.0 - lb) / (1.0 + jnp.exp(x))
    return g, kk


def _gla_chain(q, kk, v, ex, decay, level_is, row_bit, st_ref, slot, *, d):
    c = GLA_C
    v = v.astype(BF16)
    att = jnp.where(level_is[-1], _dot_nt(q.astype(BF16), kk.astype(BF16)), 0.0)
    for lv in range(GLA_LEVELS):
        ex_l = ex[lv * c:(lv + 1) * c]
        half = 1 << lv
        if half < SUBLANES:
            is_q = row_bit[lv] if d == 0 else jnp.logical_not(row_bit[lv])
            scaled = jnp.where(is_q, q, kk) * ex_l
            ql = jnp.where(is_q, scaled, 0.0)
            kl = jnp.where(is_q, 0.0, scaled)
            att = jnp.where(level_is[lv], _dot_nt(ql.astype(BF16), kl.astype(BF16)), att)
        else:
            zeros = jnp.zeros((half, DH), F32)
            blocks = [slice(blk * half, (blk + 1) * half) for blk in range(c // half)]
            q_side = [(blk & 1) == 1 - d for blk in range(c // half)]
            ql = jnp.concatenate([q[rows] * ex_l[rows] for rows, qs in zip(blocks, q_side) if qs], axis=0)
            kl = jnp.concatenate([zeros if qs else kk[rows] * ex_l[rows] for rows, qs in zip(blocks, q_side)], axis=0)
            prod = _dot_nt(ql.astype(BF16), kl.astype(BF16))
            parts, used = [], 0
            for rows, qs in zip(blocks, q_side):
                if qs:
                    parts.append(jnp.where(level_is[lv][rows], prod[used:used + half], att[rows]))
                    used += half
                else:
                    parts.append(att[rows])
            att = jnp.concatenate(parts, axis=0)

    q_in = (q * ex[GLA_LEVELS * c:(GLA_LEVELS + 1) * c]).astype(BF16)
    k_out = (kk * ex[(GLA_LEVELS + 1) * c:]).astype(BF16)
    st = st_ref[slot]
    o = _dot(att.astype(BF16), v) + _dot_nt(q_in, st.astype(BF16))
    st_ref[slot] = st * decay + _dot_tn(v, k_out)
    return o


def _gla_body(qf_ref, ff_ref, vf_ref, qb_ref, fb_ref, vb_ref, lb_ref, msk_ref, lvl_ref, of_ref, ob_ref, st_ref,
              *, layer):
    c = GLA_C

    @pl.when(pl.program_id(2) == 0)
    def _():
        st_ref[...] = jnp.zeros_like(st_ref)

    lvl = lvl_ref[...]
    level_is = [lvl == lv for lv in range(GLA_LEVELS)] + [lvl == -1]
    row = lax.broadcasted_iota(jnp.int32, (c, DH), 0)
    row_bit = [((row >> lv) & 1) == 1 for lv in range(GLA_LEVELS)]
    for d, (q_ref, f_ref, v_ref, o_ref) in enumerate(((qf_ref, ff_ref, vf_ref, of_ref),
                                                      (qb_ref, fb_ref, vb_ref, ob_ref))):
        g, kk = _gla_gates(f_ref[...], lb_ref[d], layer)
        g = g * LOG2_E
        g_hi = g.astype(BF16)
        g_lo = (g - g_hi.astype(F32)).astype(BF16)
        ex = jnp.exp2(_dot(msk_ref[d], jnp.concatenate([g_hi, g_lo], axis=0)))
        decay = jnp.exp2(jnp.sum(g, axis=0, keepdims=True))
        for hh in range(GLA_HB):
            cols = slice(hh * DH, (hh + 1) * DH)
            o_ref[:, cols] = _gla_chain(q_ref[:, cols], kk[:, cols], v_ref[:, cols], ex[:, cols], decay[:, cols],
                                        level_is, row_bit, st_ref, d * GLA_HB + hh, d=d)


def gla(p, lb_raw, masks, lvl, b, t, tc, layer):
    c = GLA_C
    m = p.shape[0]
    n_ctx_chunks = tc // c
    n_lat_chunks = t // c
    n_chunks = n_ctx_chunks + n_lat_chunks
    ctx0 = (b * t) // c
    n_even = lb_raw.shape[1]
    wb = GLA_HB * DH

    def blk(bi, i, d):
        jc = i if d == 0 else n_ctx_chunks - 1 - i
        jl = i - n_ctx_chunks if d == 0 else n_chunks - 1 - i
        return jnp.where(i < n_ctx_chunks, ctx0 + bi * n_ctx_chunks + jc, bi * n_lat_chunks + jl)

    def spec(section, d):
        col0 = (P_FRONT + section * H * DH) // wb
        return pl.BlockSpec((c, wb), lambda bi, hh, i: (blk(bi, i, d), col0 + hh))

    out_spec = lambda d: pl.BlockSpec((c, wb), lambda bi, hh, i: (blk(bi, i, d), hh))
    return pl.pallas_call(
        functools.partial(_gla_body, layer=layer),
        grid=(b, H // GLA_HB, n_chunks),
        in_specs=[spec(0, 0), spec(1, 0), spec(3, 0), spec(0, 1), spec(2, 1), spec(3, 1),
                  pl.BlockSpec((2, n_even, wb), lambda bi, hh, i: (0, 0, hh)),
                  pl.BlockSpec((2, (GLA_LEVELS + 2) * c, 2 * c), lambda bi, hh, i: (0, 0, 0)),
                  pl.BlockSpec((c, c), lambda bi, hh, i: (0, 0))],
        out_specs=[out_spec(0), out_spec(1)],
        out_shape=[jax.ShapeDtypeStruct((m, H * DH), F32)] * 2,
        scratch_shapes=[pltpu.VMEM((2 * GLA_HB, DH, DH), F32)],
        compiler_params=_cp("parallel", "parallel", "arbitrary"),
        name="hgrn_scan",
    )(p, p, p, p, p, p, lb_raw, masks, lvl)


def _hg_merge_body(of_ref, ob_ref, g_ref, n_ref, y_ref):
    o = of_ref[...] + ob_ref[...]
    gate = g_ref[...]
    gain = n_ref[...]
    for h in range(H):
        sl = slice(h * DH, (h + 1) * DH)
        gh = gate[:, sl]
        y_ref[:, sl] = (_rms(o[:, sl]) * gain * (gh * _sigmoid(gh))).astype(BF16)


def hg_merge(o_f, o_b, p, hg_norm):
    m = p.shape[0]
    tm = 512
    w = H * DH
    row = lambda i: (i, 0)
    return pl.pallas_call(
        _hg_merge_body,
        grid=(m // tm,),
        in_specs=[pl.BlockSpec((tm, w), row),
                  pl.BlockSpec((tm, w), row),
                  pl.BlockSpec((tm, w), lambda i: (i, (P_WIDTH - w) // w)),
                  pl.BlockSpec((1, DH), lambda i: (0, 0))],
        out_specs=pl.BlockSpec((tm, w), row),
        out_shape=jax.ShapeDtypeStruct((m, w), BF16),
        compiler_params=_cp("parallel"),
        name="hgrn_merge",
    )(o_f, o_b, p, hg_norm.reshape(1, DH))


RES_TM = 512
RES_TN = 512
RES_ROWS = 128
RES_RESIDENT_BYTES = 8 * 1024 * 1024


def _mm_res_body(*refs, groups, gate, shift, scale, nj, has_bias, has_next):
    refs = list(refs)
    a_refs = [[refs.pop(0) for _ in pieces] for _, pieces in groups]
    w_ref = refs.pop(0)
    b_ref = refs.pop(0) if has_bias else None
    x_ref, gp_ref, mc_ref = refs.pop(0), refs.pop(0), refs.pop(0)
    gn_ref, mn_ref = (refs.pop(0), refs.pop(0)) if has_next else (None, None)
    xo_ref = refs.pop(0)
    uo_ref = refs.pop(0) if has_next else None
    i = pl.program_id(0)
    j = pl.program_id(1)
    tn = w_ref.shape[1]
    cols = pl.ds(pl.multiple_of(j * tn, tn), tn)

    w = w_ref[...]
    koff = 0
    for gi, ((kw, pieces), piece_refs) in enumerate(zip(groups, a_refs)):
        wk = w[koff:koff + kw]
        koff += kw
        for a_ref, (r0, nr) in zip(piece_refs, pieces):
            def project(a_ref=a_ref, wk=wk, gi=gi):
                part = _dot(a_ref[...], wk)
                if gi == 0:
                    xo_ref[:, cols] = part
                else:
                    xo_ref[:, cols] += part

            if len(pieces) == 1:
                project()
            else:
                pl.when((i >= r0) & (i < r0 + nr))(project)

    @pl.when(j == nj - 1)
    def _():
        mc = mc_ref[0]
        gate_gain = mc[:, gate * D:(gate + 1) * D] * gp_ref[...]
        if has_next:
            mn = mn_ref[0]
            scale_gain = gn_ref[...] * (1.0 + mn[:, scale * D:(scale + 1) * D])
            shift_v = mn[:, shift * D:(shift + 1) * D]

        def rows_pass(r, carry):
            rows = pl.ds(pl.multiple_of(r * RES_ROWS, RES_ROWS), RES_ROWS)
            y = xo_ref[rows, :]
            if has_bias:
                y = y + b_ref[...]
            xn = x_ref[rows, :] + _rms(y) * gate_gain
            xo_ref[rows, :] = xn
            if has_next:
                uo_ref[rows, :] = (_rms(xn) * scale_gain + shift_v).astype(BF16)
            return carry

        lax.fori_loop(0, xo_ref.shape[0] // RES_ROWS, rows_pass, 0)


def matmul_residual(groups, w, layer, bias, x, m, g_post, mods_cur, gate, n_lat, seq, tm, nxt=None):
    kdim = w.shape[1]
    resident = kdim * D * w.dtype.itemsize <= RES_RESIDENT_BYTES
    tn = D if resident else RES_TN
    nj = D // tn
    mrow = lambda i, j: (_mod_row(i, tm, n_lat, seq), 0, 0)
    row = lambda i, j: (i, 0)
    fixed = lambda i, j: (0, 0)

    def piece_spec(arr, r0, nr):
        return pl.BlockSpec((tm, arr.shape[1]), lambda i, j: (jnp.clip(i - r0, 0, nr - 1), 0))

    in_specs = [piece_spec(*pc) for g in groups for pc in g]
    w_mode = {"pipeline_mode": pl.Buffered(1)} if resident else {}
    in_specs.append(pl.BlockSpec((None, kdim, tn), lambda i, j: (layer, 0, j), **w_mode))
    args = [pc[0] for g in groups for pc in g] + [w]
    if bias is not None:
        in_specs.append(pl.BlockSpec((1, D), fixed))
        args.append(bias.reshape(1, D))
    in_specs += [pl.BlockSpec((tm, D), row),
                 pl.BlockSpec((1, D), fixed),
                 pl.BlockSpec((1, 1, 6 * D), mrow)]
    args += [x, g_post.reshape(1, D), mods_cur]
    out_specs = [pl.BlockSpec((tm, D), row)]
    out_shape = [jax.ShapeDtypeStruct((m, D), F32)]
    shift = scale = 0
    if nxt is not None:
        g_next, mods_next, shift, scale = nxt
        in_specs += [pl.BlockSpec((1, D), fixed), pl.BlockSpec((1, 1, 6 * D), mrow)]
        args += [g_next.reshape(1, D), mods_next]
        out_specs.append(pl.BlockSpec((tm, D), row))
        out_shape.append(jax.ShapeDtypeStruct((m, D), BF16))
    group_meta = tuple((g[0][0].shape[1], tuple(pc[1:] for pc in g)) for g in groups)
    assert sum(kw for kw, _ in group_meta) == kdim
    outs = pl.pallas_call(
        functools.partial(_mm_res_body, groups=group_meta, gate=gate, shift=shift, scale=scale,
                          nj=nj, has_bias=bias is not None, has_next=nxt is not None),
        grid=(m // tm, nj),
        in_specs=in_specs,
        out_specs=out_specs,
        out_shape=out_shape,
        compiler_params=_cp("parallel", "arbitrary"),
        name="proj_residual",
    )(*args)
    return (outs[0], outs[1]) if nxt is not None else (outs[0], None)


GATED_HALO = 16


def _mm_gated_body(*refs, conv, silu, lat_tiles, seq_lat, seq_ctx):
    if conv:
        x_ref, xp_ref, xn_ref, wa_ref, wg_ref, ca_ref, cg_ref, ba_ref, bg_ref, o_ref = refs
    else:
        x_ref, wa_ref, wg_ref, ba_ref, bg_ref, o_ref = refs
    x = x_ref[...]
    wa = wa_ref[...].astype(BF16)
    wg = wg_ref[...].astype(BF16)
    za = _dot(x, wa)
    zg = _dot(x, wg)
    if conv:
        i = pl.program_id(0)
        tm, tn = za.shape
        halo = jnp.concatenate([xp_ref[...], xn_ref[...]], axis=0)
        ha = _dot(halo, wa)
        hg = _dot(halo, wg)
        is_ctx = i >= lat_tiles
        seq_m1 = jnp.where(is_ctx, seq_ctx - 1, seq_lat - 1)
        starts_seq = ((i * tm) & seq_m1) == 0
        ends_seq = (((i + 1) * tm) & seq_m1) == 0
        sub = lax.broadcasted_iota(jnp.int32, (SUBLANES, tn), 0)
        top = sub == 0
        bottom = sub == SUBLANES - 1
        inner = range(seq_ctx, tm, seq_ctx)
        g = SUBLANES

        def conv3(z, h, cw):
            up = pltpu.roll(z, 1, 0)
            dn = pltpu.roll(z, tm - 1, 0)
            before = jnp.where(starts_seq, 0.0, h[GATED_HALO - 1:GATED_HALO])
            after = jnp.where(ends_seq, 0.0, h[GATED_HALO:GATED_HALO + 1])
            up_parts, dn_parts = [jnp.where(top, before, up[:g])], []
            up_lo, dn_lo = g, 0
            for r in inner:
                up_parts += [up[up_lo:r], jnp.where(jnp.logical_and(top, is_ctx), 0.0, up[r:r + g])]
                dn_parts += [dn[dn_lo:r - g], jnp.where(jnp.logical_and(bottom, is_ctx), 0.0, dn[r - g:r])]
                up_lo, dn_lo = r + g, r
            up_parts.append(up[up_lo:])
            dn_parts += [dn[dn_lo:tm - g], jnp.where(bottom, after, dn[tm - g:])]
            up = jnp.concatenate(up_parts, axis=0)
            dn = jnp.concatenate(dn_parts, axis=0)
            return cw[0:1] * up + cw[1:2] * z + cw[2:3] * dn

        za = conv3(za, ha, ca_ref[...])
        zg = conv3(zg, hg, cg_ref[...])
    za = za + ba_ref[...]
    zg = zg + bg_ref[...]
    sg = _sigmoid(zg)
    o_ref[...] = (za * (zg * sg if silu else sg)).astype(o_ref.dtype)


def matmul_gated(x, w, layer, conv_w, bias, *, m, tm, tn, n_lat, seq_lat, seq_ctx, silu, out_dtype, name):
    kdim = x.shape[1]
    n = w.shape[2] // 2
    nj = n // tn
    first = lambda i, j: (layer, 0, j)
    second = lambda i, j: (layer, 0, j + nj)
    bias = bias.reshape(bias.shape[0], 1, 2 * n)
    in_specs = [pl.BlockSpec((tm, kdim), lambda i, j: (i, 0))]
    args = [x]
    if conv_w is not None:
        hb = tm // GATED_HALO
        last_hb = x.shape[0] // GATED_HALO - 1
        in_specs += [pl.BlockSpec((GATED_HALO, kdim), lambda i, j: (jnp.maximum(i * hb - 1, 0), 0)),
                     pl.BlockSpec((GATED_HALO, kdim), lambda i, j: (jnp.minimum((i + 1) * hb, last_hb), 0))]
        args += [x, x]
    in_specs += [pl.BlockSpec((None, kdim, tn), first), pl.BlockSpec((None, kdim, tn), second)]
    args += [w, w]
    if conv_w is not None:
        in_specs += [pl.BlockSpec((None, conv_w.shape[1], tn), first),
                     pl.BlockSpec((None, conv_w.shape[1], tn), second)]
        args += [conv_w, conv_w]
    in_specs += [pl.BlockSpec((None, 1, tn), first), pl.BlockSpec((None, 1, tn), second)]
    args += [bias, bias]
    return pl.pallas_call(
        functools.partial(_mm_gated_body, conv=conv_w is not None, silu=silu, lat_tiles=n_lat // tm,
                          seq_lat=seq_lat, seq_ctx=seq_ctx),
        grid=(m // tm, nj),
        in_specs=in_specs,
        out_specs=pl.BlockSpec((tm, tn), lambda i, j: (i, j)),
        out_shape=jax.ShapeDtypeStruct((m, n), out_dtype),
        compiler_params=_cp("parallel", "parallel"),
        name=name,
    )(*args)


def _dwconv_ln_body(prev_ref, cur_ref, next_ref, w_ref, b_ref, g_ref, beta_ref, o_ref, ext_ref, sh_ref, acc_ref,
                    *, tiles_per_seq, n_lat_tiles):
    i = pl.program_id(0)
    tm = cur_ref.shape[0]
    hw = CONV_HALO
    is_lat = i < n_lat_tiles
    j = i % tiles_per_seq
    has_prev = jnp.logical_and(is_lat, j > 0).astype(F32)
    has_next = jnp.logical_and(is_lat, j < tiles_per_seq - 1).astype(F32)
    ext_ref[0:hw, :] = prev_ref[tm - hw:, :] * has_prev
    ext_ref[hw:hw + tm, :] = cur_ref[...]
    ext_ref[hw + tm:, :] = next_ref[0:hw, :] * has_next

    sh_rows = sh_ref.shape[1]

    def lane_block(cb, carry):
        col = pl.multiple_of(cb * DH, DH)
        wv = w_ref[:, pl.ds(col, DH)]
        for s in range(SUBLANES):
            sh_ref[s] = ext_ref[s:s + sh_rows, pl.ds(col, DH)]
        for r0 in range(0, tm, 128):
            acc = jnp.zeros((128, DH), F32)
            for k in range(CONV_W):
                off = r0 + k + hw - (CONV_W - 1) // 2
                s = off % SUBLANES
                acc = acc + sh_ref[s, off - s:off - s + 128, :] * wv[k:k + 1]
            acc_ref[r0:r0 + 128, pl.ds(col, DH)] = acc
        return carry

    lax.fori_loop(0, D // DH, lane_block, 0)
    hcv = acc_ref[...] + b_ref[...]
    mu = jnp.mean(hcv, axis=-1, keepdims=True)
    xc = hcv - mu
    var = jnp.mean(xc * xc, axis=-1, keepdims=True)
    y = xc * lax.rsqrt(var + EPS) * g_ref[...] + beta_ref[...]
    o_ref[...] = (y * _sigmoid(y)).astype(BF16)


def dwconv_ln(hx, w_dw, b_dw, ln_g, ln_b, n_lat, seq):
    m = hx.shape[0]
    tm = 256
    nt = m // tm
    fixed = lambda i: (0, 0)
    vec = lambda a: a.reshape(1, D)
    return pl.pallas_call(
        functools.partial(_dwconv_ln_body, tiles_per_seq=seq // tm, n_lat_tiles=n_lat // tm),
        grid=(nt,),
        in_specs=[pl.BlockSpec((tm, D), lambda i: (jnp.maximum(i - 1, 0), 0)),
                  pl.BlockSpec((tm, D), lambda i: (i, 0)),
                  pl.BlockSpec((tm, D), lambda i: (jnp.minimum(i + 1, nt - 1), 0)),
                  pl.BlockSpec((CONV_W, D), fixed),
                  pl.BlockSpec((1, D), fixed),
                  pl.BlockSpec((1, D), fixed),
                  pl.BlockSpec((1, D), fixed)],
        out_specs=pl.BlockSpec((tm, D), lambda i: (i, 0)),
        out_shape=jax.ShapeDtypeStruct((m, D), BF16),
        scratch_shapes=[pltpu.VMEM((tm + 2 * CONV_HALO, D), F32),
                        pltpu.VMEM((SUBLANES, tm + 2 * CONV_HALO - SUBLANES, DH), F32),
                        pltpu.VMEM((tm, D), F32)],
        compiler_params=_cp("parallel"),
        name="dwconv_ln",
    )(hx, hx, hx, w_dw, vec(b_dw), vec(ln_g), vec(ln_b))


def _rope_tables(b, t, tc):
    rows = t // GRID_W
    row = jnp.repeat(jnp.arange(rows), GRID_W).astype(F32)
    col = jnp.tile(jnp.arange(GRID_W), rows).astype(F32)
    pairs = ROPE // 4
    freqs = ROPE_BASE ** (-jnp.arange(pairs, dtype=F32) / pairs)
    ar = row[:, None] * freqs
    ac = col[:, None] * freqs
    cos = jnp.concatenate([jnp.cos(ar), jnp.cos(ar), jnp.cos(ac), jnp.cos(ac), jnp.ones((t, DH - ROPE), F32)], axis=1)
    sin = jnp.concatenate([-jnp.sin(ar), jnp.sin(ar), -jnp.sin(ac), jnp.sin(ac), jnp.zeros((t, DH - ROPE), F32)], axis=1)
    cos = jnp.concatenate([jnp.tile(cos, (b, 1)), jnp.ones((b * tc, DH), F32)], axis=0)
    sin = jnp.concatenate([jnp.tile(sin, (b, 1)), jnp.zeros((b * tc, DH), F32)], axis=0)
    return cos, sin


def _pick(n, *cands):
    for cand in cands:
        if n % cand == 0:
            return cand
    raise ValueError(f"no tile for {n}")


def kernel(x, c, ctx, c_ctx, w_mod, b_mod, norm_gains, w_in_ab, mla_q_norm, w_q_up, mla_kv_norm, w_kv_up, hgrn_lb, hgrn_norm, w_out_ab, conv_w_pw1, conv_b_pw1, conv_w_dw, conv_b_dw, conv_ln_g, conv_ln_b, conv_w_pw2, conv_b_pw2, ffn_w_up, ffn_w_conv, ffn_b_conv, ffn_w_down):
    b, t, _ = x.shape
    tc = ctx.shape[1]
    depth = w_mod.shape[0]
    n_lat = b * t
    n_ctx = b * tc
    assert x.shape[2] == D and depth % 2 == 0 and b < 8
    assert t % 512 == 0 and tc % 256 == 0 and n_ctx % 512 == 0 and t % n_ctx == 0
    assert t & (t - 1) == 0 and tc & (tc - 1) == 0
    tg = _pick(n_ctx, 1024, 512)
    tr = RES_TM
    lat_tiles = n_lat // tr
    ctx_tiles = n_ctx // tr
    w_out_bf, w_pw2_bf, w_down_bf = w_out_ab.astype(BF16), conv_w_pw2.astype(BF16), ffn_w_down.astype(BF16)
    w_in_p = repack_in_proj(w_in_ab)

    hcur = jnp.concatenate([x.reshape(n_lat, D), ctx.reshape(n_ctx, D)], axis=0)
    cc = jnp.concatenate([c, c_ctx[None, :], jnp.zeros((7 - b, D), F32)], axis=0)
    mods = modulation(cc, w_mod, b_mod).reshape(depth, 8, 1, 6 * D)
    cos, sin = _rope_tables(b, t, tc)
    masks_np, lvl_np = _gla_tables()
    masks = jnp.asarray(masks_np, BF16)
    lvl = jnp.asarray(lvl_np)

    u = norm_mod(hcur, norm_gains[0, 0], mods[0], n_lat, t, shift=0, scale=1)
    for l in range(depth):
        last = l == depth - 1
        j = l // 2
        m = n_lat if last else n_lat + n_ctx
        row_tiles = m // tr
        g_pre1, g_post1, g_pre2, g_post2 = norm_gains[l]
        if l % 2 == 0:
            wq =jnp.pad(w_q_up[j].reshape(QL, H, DH + ROPE), ((0, 0), (0, 0), (0, DH - ROPE)))
            wq = wq.reshape(QL, 2 * H * DH).astype(BF16)
            wkv = w_kv_up[j].reshape(KVL, H, 2 * DH)
            wkv = jnp.concatenate([wkv[:, :, :DH].reshape(KVL, H * DH), wkv[:, :, DH:].reshape(KVL, H * DH)], axis=1)
            p = matmul(u, w_in_p, j, _pick(m, 1024, 512), 512)
            q, k, v = mla_prep(p, mla_q_norm[j], wq, mla_kv_norm[j], wkv.astype(BF16), cos, sin)
            att_lat = attention_latent(q, k, v, b, t, tc)
            att_ctx = attention_context(q, k, v, b, t, tc)
            o_f, o_b = gla(p, hgrn_lb, masks, lvl, b, t, tc, j)
            hg = hg_merge(o_f, o_b, p, hgrn_norm[j])
            att_pieces = [(att_lat, 0, lat_tiles)] + ([] if last else [(att_ctx, lat_tiles, ctx_tiles)])
            groups = [att_pieces, [(hg, 0, row_tiles)]]
            w_o, b_o = w_out_bf, None
        else:
            glu = matmul_gated(u, conv_w_pw1, j, None, conv_b_pw1, m=m, tm=tg, tn=512, n_lat=n_lat, seq_lat=t,
                               seq_ctx=tc, silu=False, out_dtype=F32, name="conformer_glu")
            mix = dwconv_ln(glu, conv_w_dw[j], conv_b_dw[j], conv_ln_g[j], conv_ln_b[j], n_lat, t)
            groups = [[(mix, 0, row_tiles)]]
            w_o, b_o = w_pw2_bf, conv_b_pw2[j]
        hcur, u = matmul_residual(groups, w_o, j, b_o, hcur, m, g_post1, mods[l], 2, n_lat, t, tr,
                                  nxt=(g_pre2, mods[l], 3, 4))
        act = matmul_gated(u, ffn_w_up, l, ffn_w_conv, ffn_b_conv, m=m, tm=tg, tn=512, n_lat=n_lat, seq_lat=t,
                           seq_ctx=tc, silu=True, out_dtype=BF16, name="ffn_up")
        nxt = None if last else (norm_gains[l + 1, 0], mods[l + 1], 0, 1)
        hcur, u = matmul_residual([[(act, 0, row_tiles)]], w_down_bf, l, None, hcur, m, g_post2, mods[l], 5,
                                  n_lat, t, tr, nxt=nxt)
    return hcur.reshape(b, t, D)
```

```python
import functools

import numpy as np
import jax
import jax.numpy as jnp
from jax import lax
from jax.experimental import pallas as pl
from jax.experimental.pallas import tpu as pltpu

F32 = jnp.float32
BF16 = jnp.bfloat16

D = 2048
H = D // 256
DH = 128
ROPE = 64
QL = D // 4
KVL = D // 8
DFF = (11 * D) // 4
GRID_W = 64
ROPE_BASE = 10000.0
CONV_W = 31
CONV_HALO = 16
EPS = 1e-6
ATTN_SCALE = (DH + ROPE) ** -0.5
LOG2_E = float(np.log2(np.e))
Q_SCALE = ATTN_SCALE * LOG2_E
ATTN_HB = 4
P_FRONT = 1024
P_WIDTH = P_FRONT + 5 * H * DH
GLA_C = 128
GLA_LEVELS = 7
GLA_HB = 8
SUBLANES = 8
VMEM_LIMIT = 56 * 1024 * 1024


def _cp(*sem):
    return pltpu.CompilerParams(dimension_semantics=sem, vmem_limit_bytes=VMEM_LIMIT)


def _sigmoid(x):
    return 1.0 / (1.0 + jnp.exp(-x))


def _rms(x):
    return x * lax.rsqrt(jnp.mean(x * x, axis=-1, keepdims=True) + EPS)


def _dot(a, b):
    return jnp.dot(a, b, preferred_element_type=F32)


def _dot_nt(a, b):
    return lax.dot_general(a, b, (((1,), (1,)), ((), ())), preferred_element_type=F32)


def _dot_tn(a, b):
    return lax.dot_general(a, b, (((0,), (0,)), ((), ())), preferred_element_type=F32)


def _mod_row(i, tm, n_lat, seq):
    return jnp.where(i < n_lat // tm, i // (seq // tm), n_lat // seq)


def _mod_body(c_ref, w_ref, b_ref, o_ref):
    c = c_ref[...]
    s = (c * _sigmoid(c)).astype(BF16)
    o_ref[0] = _dot(s, w_ref[0].astype(BF16)) + b_ref[0]


def modulation(cc, w_mod, b_mod):
    depth, _, n = w_mod.shape
    rows = cc.shape[0]
    tn = 1024
    return pl.pallas_call(
        _mod_body,
        grid=(depth, n // tn),
        in_specs=[pl.BlockSpec((rows, D), lambda l, j: (0, 0)),
                  pl.BlockSpec((1, D, tn), lambda l, j: (l, 0, j)),
                  pl.BlockSpec((1, 1, tn), lambda l, j: (l, 0, j))],
        out_specs=pl.BlockSpec((1, rows, tn), lambda l, j: (l, 0, j)),
        out_shape=jax.ShapeDtypeStruct((depth, rows, n), F32),
        compiler_params=_cp("parallel", "parallel"),
        name="modulation",
    )(cc, w_mod, b_mod.reshape(depth, 1, n))


def _norm_mod_body(x_ref, g_ref, m_ref, o_ref, *, shift, scale):
    m = m_ref[0]
    y = _rms(x_ref[...]) * g_ref[...]
    o_ref[...] = (y * (1.0 + m[:, scale * D:(scale + 1) * D]) + m[:, shift * D:(shift + 1) * D]).astype(BF16)


def norm_mod(x, gain, mods, n_lat, seq, shift, scale):
    m = x.shape[0]
    tm = 256
    return pl.pallas_call(
        functools.partial(_norm_mod_body, shift=shift, scale=scale),
        grid=(m // tm,),
        in_specs=[pl.BlockSpec((tm, D), lambda i: (i, 0)),
                  pl.BlockSpec((1, D), lambda i: (0, 0)),
                  pl.BlockSpec((1, 1, 6 * D), lambda i: (_mod_row(i, tm, n_lat, seq), 0, 0))],
        out_specs=pl.BlockSpec((tm, D), lambda i: (i, 0)),
        out_shape=jax.ShapeDtypeStruct((m, D), BF16),
        compiler_params=_cp("parallel"),
        name="norm_mod",
    )(x, gain.reshape(1, D), mods)


def _repack_in_body(w_ref, o_ref):
    front = QL + KVL + ROPE
    w = w_ref[...]
    o_ref[:, :front] = w[:, :front].astype(BF16)
    o_ref[:, front:P_FRONT] = jnp.zeros((w.shape[0], P_FRONT - front), BF16)
    o_ref[:, P_FRONT:] = w[:, front:].astype(BF16)


def repack_in_proj(w_in):
    layers, k, n = w_in.shape
    tr = 256
    return pl.pallas_call(
        _repack_in_body,
        grid=(layers, k // tr),
        in_specs=[pl.BlockSpec((None, tr, n), lambda l, i: (l, i, 0))],
        out_specs=pl.BlockSpec((None, tr, P_WIDTH), lambda l, i: (l, i, 0)),
        out_shape=jax.ShapeDtypeStruct((layers, k, P_WIDTH), BF16),
        compiler_params=_cp("parallel", "parallel"),
        name="repack_in_proj",
    )(w_in)


def _mm_body(x_ref, w_ref, o_ref):
    o_ref[...] = _dot(x_ref[...], w_ref[...])


def matmul(x, w, layer, tm, tn):
    m, k = x.shape
    n = w.shape[2]
    return pl.pallas_call(
        _mm_body,
        grid=(m // tm, n // tn),
        in_specs=[pl.BlockSpec((tm, k), lambda i, j: (i, 0)),
                  pl.BlockSpec((None, k, tn), lambda i, j: (layer, 0, j))],
        out_specs=pl.BlockSpec((tm, tn), lambda i, j: (i, j)),
        out_shape=jax.ShapeDtypeStruct((m, n), F32),
        compiler_params=_cp("parallel", "parallel"),
        name="in_proj",
    )(x, w)


def _mla_prep_body(p_ref, qn_ref, wq_ref, kvn_ref, wkv_ref, cos_ref, sin_ref, q_ref, k_ref, v_ref):
    p = p_ref[...]
    tm = p.shape[0]
    cos = cos_ref[...]
    sin = sin_ref[...]
    lane = lax.broadcasted_iota(jnp.int32, (tm, DH), 1)
    first_half = (lane & 31) < 16

    def rope(x):
        partner = jnp.where(first_half, pltpu.roll(x, DH - 16, 1), pltpu.roll(x, 16, 1))
        return x * cos + partner * sin

    cq = (_rms(p[:, :QL]) * qn_ref[...]).astype(BF16)
    ckv = (_rms(p[:, QL:QL + KVL]) * kvn_ref[...]).astype(BF16)
    kr = rope(p[:, QL + KVL:QL + KVL + DH]).astype(BF16)
    q = _dot(cq, wq_ref[...])
    kv = _dot(ckv, wkv_ref[...])
    for h in range(H):
        lo = 2 * DH * h
        q_ref[:, lo:lo + DH] = (q[:, lo:lo + DH] * Q_SCALE).astype(BF16)
        q_ref[:, lo + DH:lo + 2 * DH] = (rope(q[:, lo + DH:lo + 2 * DH]) * Q_SCALE).astype(BF16)
        k_ref[:, lo:lo + DH] = kv[:, DH * h:DH * (h + 1)].astype(BF16)
        k_ref[:, lo + DH:lo + 2 * DH] = kr
    v_ref[...] = kv[:, H * DH:].astype(BF16)


def mla_prep(p, q_norm, wq, kv_norm, wkv, cos, sin):
    m = p.shape[0]
    tm = 512
    row = lambda i: (i, 0)
    fixed = lambda i: (0, 0)
    return pl.pallas_call(
        _mla_prep_body,
        grid=(m // tm,),
        in_specs=[pl.BlockSpec((tm, P_FRONT), row),
                  pl.BlockSpec((1, QL), fixed),
                  pl.BlockSpec((QL, 2 * H * DH), fixed),
                  pl.BlockSpec((1, KVL), fixed),
                  pl.BlockSpec((KVL, 2 * H * DH), fixed),
                  pl.BlockSpec((tm, DH), row),
                  pl.BlockSpec((tm, DH), row)],
        out_specs=[pl.BlockSpec((tm, 2 * H * DH), row),
                   pl.BlockSpec((tm, 2 * H * DH), row),
                   pl.BlockSpec((tm, H * DH), row)],
        out_shape=[jax.ShapeDtypeStruct((m, 2 * H * DH), BF16),
                   jax.ShapeDtypeStruct((m, 2 * H * DH), BF16),
                   jax.ShapeDtypeStruct((m, H * DH), BF16)],
        compiler_params=_cp("parallel"),
        name="mla_prep",
    )(p, q_norm.reshape(1, QL), wq, kv_norm.reshape(1, KVL), wkv, cos, sin)


def _attn_lat_body(q_ref, kl_ref, vl_ref, kc_ref, vc_ref, o_ref):
    for hh in range(ATTN_HB):
        qk = slice(hh * 2 * DH, (hh + 1) * 2 * DH)
        vo = slice(hh * DH, (hh + 1) * DH)
        q = q_ref[:, qk]
        sl = _dot_nt(q, kl_ref[:, qk])
        sc = _dot_nt(q, kc_ref[:, qk])
        mx = jnp.maximum(jnp.max(sl, axis=-1, keepdims=True), jnp.max(sc, axis=-1, keepdims=True))
        el = jnp.exp2(sl - mx)
        ec = jnp.exp2(sc - mx)
        den = jnp.sum(el, axis=-1, keepdims=True) + jnp.sum(ec, axis=-1, keepdims=True)
        o = _dot(el.astype(BF16), vl_ref[:, vo]) + _dot(ec.astype(BF16), vc_ref[:, vo])
        o_ref[:, vo] = (o / den).astype(BF16)


def _attn_ctx_body(q_ref, kc_ref, vc_ref, o_ref):
    sc = _dot_nt(q_ref[...], kc_ref[...])
    ec = jnp.exp2(sc - jnp.max(sc, axis=-1, keepdims=True))
    o = _dot(ec.astype(BF16), vc_ref[...])
    o_ref[...] = (o / jnp.sum(ec, axis=-1, keepdims=True)).astype(BF16)


def attention_latent(q, k, v, b, t, tc):
    n_lat = b * t
    tq = 512
    nq = t // tq
    cb = n_lat // tc
    wqk = ATTN_HB * 2 * DH
    wv = ATTN_HB * DH
    return pl.pallas_call(
        _attn_lat_body,
        grid=(b, H // ATTN_HB, nq),
        in_specs=[pl.BlockSpec((tq, wqk), lambda bi, h, qi: (bi * nq + qi, h)),
                  pl.BlockSpec((t, wqk), lambda bi, h, qi: (bi, h)),
                  pl.BlockSpec((t, wv), lambda bi, h, qi: (bi, h)),
                  pl.BlockSpec((tc, wqk), lambda bi, h, qi: (cb + bi, h)),
                  pl.BlockSpec((tc, wv), lambda bi, h, qi: (cb + bi, h))],
        out_specs=pl.BlockSpec((tq, wv), lambda bi, h, qi: (bi * nq + qi, h)),
        out_shape=jax.ShapeDtypeStruct((n_lat, H * DH), BF16),
        compiler_params=_cp("parallel", "parallel", "parallel"),
        name="attn_latent",
    )(q, k, v, k, v)


def attention_context(q, k, v, b, t, tc):
    cb = (b * t) // tc
    return pl.pallas_call(
        _attn_ctx_body,
        grid=(b, H),
        in_specs=[pl.BlockSpec((tc, 2 * DH), lambda bi, h: (cb + bi, h)),
                  pl.BlockSpec((tc, 2 * DH), lambda bi, h: (cb + bi, h)),
                  pl.BlockSpec((tc, DH), lambda bi, h: (cb + bi, h))],
        out_specs=pl.BlockSpec((tc, DH), lambda bi, h: (bi, h)),
        out_shape=jax.ShapeDtypeStruct((b * tc, H * DH), BF16),
        compiler_params=_cp("parallel", "parallel"),
        name="attn_context",
    )(q, k, v)


def _gla_tables():
    c = GLA_C
    t = np.arange(c)[:, None]
    r = np.arange(c)[None, :]
    mats = []
    for lv in range(GLA_LEVELS):
        same = (t >> (lv + 1)) == (r >> (lv + 1))
        bt = (t >> lv) & 1
        br = (r >> lv) & 1
        upper = same & (bt == 1) & (br == 1) & (r <= t)
        lower = same & (bt == 0) & (br == 0) & (r > t)
        mats.append(upper | lower)
    mats.append(r <= t)
    mats.append(r > t)
    fwd = np.stack(mats).astype(np.float32)
    bwd = fwd[:, ::-1, ::-1]
    masks = np.stack([fwd, bwd]).reshape(2, (GLA_LEVELS + 2) * c, c)
    masks = np.concatenate([masks, masks], axis=2)
    x = t ^ r
    lvl = np.where(x == 0, -1, np.floor(np.log2(np.maximum(x, 1)))).astype(np.int32)
    return masks, lvl


def _gla_gates(x, lb_raw, layer):
    log_sig = jnp.minimum(x, 0.0) - jnp.log1p(jnp.exp(-jnp.abs(x)))
    if layer == 0:
        g = log_sig
        kk = 1.0 / (1.0 + jnp.exp(x))
    else:
        e = jnp.exp(lb_raw - jnp.max(lb_raw, axis=0, keepdims=True))
        prob = e / jnp.sum(e, axis=0, keepdims=True)
        lb = jnp.sum(prob[1:layer + 1], axis=0, keepdims=True)
        a = jnp.log(lb)
        bb = jnp.log1p(-lb) + log_sig
        g = jnp.maximum(a, bb) + jnp.log1p(jnp.exp(-jnp.abs(a - bb)))
        kk = (1.0 - lb) / (1.0 + jnp.exp(x))
    return g, kk


def _gla_chain(q, kk, v, ex, decay, level_is, row_bit, st_ref, slot, *, d):
    c = GLA_C
    v = v.astype(BF16)
    att = jnp.where(level_is[-1], _dot_nt(q.astype(BF16), kk.astype(BF16)), 0.0)
    for lv in range(GLA_LEVELS):
        ex_l = ex[lv * c:(lv + 1) * c]
        half = 1 << lv
        if half < SUBLANES:
            is_q = row_bit[lv] if d == 0 else jnp.logical_not(row_bit[lv])
            scaled = jnp.where(is_q, q, kk) * ex_l
            ql = jnp.where(is_q, scaled, 0.0)
            kl = jnp.where(is_q, 0.0, scaled)
            att = jnp.where(level_is[lv], _dot_nt(ql.astype(BF16), kl.astype(BF16)), att)
        else:
            zeros = jnp.zeros((half, DH), F32)
            blocks = [slice(blk * half, (blk + 1) * half) for blk in range(c // half)]
            q_side = [(blk & 1) == 1 - d for blk in range(c // half)]
            ql = jnp.concatenate([q[rows] * ex_l[rows] for rows, qs in zip(blocks, q_side) if qs], axis=0)
            kl = jnp.concatenate([zeros if qs else kk[rows] * ex_l[rows] for rows, qs in zip(blocks, q_side)], axis=0)
            prod = _dot_nt(ql.astype(BF16), kl.astype(BF16))
            parts, used = [], 0
            for rows, qs in zip(blocks, q_side):
                if qs:
                    parts.append(jnp.where(level_is[lv][rows], prod[used:used + half], att[rows]))
                    used += half
                else:
                    parts.append(att[rows])
            att = jnp.concatenate(parts, axis=0)

    q_in = (q * ex[GLA_LEVELS * c:(GLA_LEVELS + 1) * c]).astype(BF16)
    k_out = (kk * ex[(GLA_LEVELS + 1) * c:]).astype(BF16)
    st = st_ref[slot]
    o = _dot(att.astype(BF16), v) + _dot_nt(q_in, st.astype(BF16))
    st_ref[slot] = st * decay + _dot_tn(v, k_out)
    return o


def _gla_body(qf_ref, ff_ref, vf_ref, qb_ref, fb_ref, vb_ref, lb_ref, msk_ref, lvl_ref, of_ref, ob_ref, st_ref,
              *, layer):
    c = GLA_C

    @pl.when(pl.program_id(2) == 0)
    def _():
        st_ref[...] = jnp.zeros_like(st_ref)

    lvl = lvl_ref[...]
    level_is = [lvl == lv for lv in range(GLA_LEVELS)] + [lvl == -1]
    row = lax.broadcasted_iota(jnp.int32, (c, DH), 0)
    row_bit = [((row >> lv) & 1) == 1 for lv in range(GLA_LEVELS)]
    for d, (q_ref, f_ref, v_ref, o_ref) in enumerate(((qf_ref, ff_ref, vf_ref, of_ref),
                                                      (qb_ref, fb_ref, vb_ref, ob_ref))):
        g, kk = _gla_gates(f_ref[...], lb_ref[d], layer)
        g = g * LOG2_E
        g_hi = g.astype(BF16)
        g_lo = (g - g_hi.astype(F32)).astype(BF16)
        ex = jnp.exp2(_dot(msk_ref[d], jnp.concatenate([g_hi, g_lo], axis=0)))
        decay = jnp.exp2(jnp.sum(g, axis=0, keepdims=True))
        for hh in range(GLA_HB):
            cols = slice(hh * DH, (hh + 1) * DH)
            o_ref[:, cols] = _gla_chain(q_ref[:, cols], kk[:, cols], v_ref[:, cols], ex[:, cols], decay[:, cols],
                                        level_is, row_bit, st_ref, d * GLA_HB + hh, d=d)


def gla(p, lb_raw, masks, lvl, b, t, tc, layer):
    c = GLA_C
    m = p.shape[0]
    n_ctx_chunks = tc // c
    n_lat_chunks = t // c
    n_chunks = n_ctx_chunks + n_lat_chunks
    ctx0 = (b * t) // c
    n_even = lb_raw.shape[1]
    wb = GLA_HB * DH

    def blk(bi, i, d):
        jc = i if d == 0 else n_ctx_chunks - 1 - i
        jl = i - n_ctx_chunks if d == 0 else n_chunks - 1 - i
        return jnp.where(i < n_ctx_chunks, ctx0 + bi * n_ctx_chunks + jc, bi * n_lat_chunks + jl)

    def spec(section, d):
        col0 = (P_FRONT + section * H * DH) // wb
        return pl.BlockSpec((c, wb), lambda bi, hh, i: (blk(bi, i, d), col0 + hh))

    out_spec = lambda d: pl.BlockSpec((c, wb), lambda bi, hh, i: (blk(bi, i, d), hh))
    return pl.pallas_call(
        functools.partial(_gla_body, layer=layer),
        grid=(b, H // GLA_HB, n_chunks),
        in_specs=[spec(0, 0), spec(1, 0), spec(3, 0), spec(0, 1), spec(2, 1), spec(3, 1),
                  pl.BlockSpec((2, n_even, wb), lambda bi, hh, i: (0, 0, hh)),
                  pl.BlockSpec((2, (GLA_LEVELS + 2) * c, 2 * c), lambda bi, hh, i: (0, 0, 0)),
                  pl.BlockSpec((c, c), lambda bi, hh, i: (0, 0))],
        out_specs=[out_spec(0), out_spec(1)],
        out_shape=[jax.ShapeDtypeStruct((m, H * DH), F32)] * 2,
        scratch_shapes=[pltpu.VMEM((2 * GLA_HB, DH, DH), F32)],
        compiler_params=_cp("parallel", "parallel", "arbitrary"),
        name="hgrn_scan",
    )(p, p, p, p, p, p, lb_raw, masks, lvl)


def _hg_merge_body(of_ref, ob_ref, g_ref, n_ref, y_ref):
    o = of_ref[...] + ob_ref[...]
    gate = g_ref[...]
    gain = n_ref[...]
    for h in range(H):
        sl = slice(h * DH, (h + 1) * DH)
        gh = gate[:, sl]
        y_ref[:, sl] = (_rms(o[:, sl]) * gain * (gh * _sigmoid(gh))).astype(BF16)


def hg_merge(o_f, o_b, p, hg_norm):
    m = p.shape[0]
    tm = 512
    w = H * DH
    row = lambda i: (i, 0)
    return pl.pallas_call(
        _hg_merge_body,
        grid=(m // tm,),
        in_specs=[pl.BlockSpec((tm, w), row),
                  pl.BlockSpec((tm, w), row),
                  pl.BlockSpec((tm, w), lambda i: (i, (P_WIDTH - w) // w)),
                  pl.BlockSpec((1, DH), lambda i: (0, 0))],
        out_specs=pl.BlockSpec((tm, w), row),
        out_shape=jax.ShapeDtypeStruct((m, w), BF16),
        compiler_params=_cp("parallel"),
        name="hgrn_merge",
    )(o_f, o_b, p, hg_norm.reshape(1, DH))


RES_TM = 512
RES_TN = 512
RES_ROWS = 128
RES_RESIDENT_BYTES = 8 * 1024 * 1024


def _mm_res_body(*refs, groups, gate, shift, scale, nj, has_bias, has_next):
    refs = list(refs)
    a_refs = [[refs.pop(0) for _ in pieces] for _, pieces in groups]
    w_ref = refs.pop(0)
    b_ref = refs.pop(0) if has_bias else None
    x_ref, gp_ref, mc_ref = refs.pop(0), refs.pop(0), refs.pop(0)
    gn_ref, mn_ref = (refs.pop(0), refs.pop(0)) if has_next else (None, None)
    xo_ref = refs.pop(0)
    uo_ref = refs.pop(0) if has_next else None
    i = pl.program_id(0)
    j = pl.program_id(1)
    tn = w_ref.shape[1]
    cols = pl.ds(pl.multiple_of(j * tn, tn), tn)

    w = w_ref[...]
    koff = 0
    for gi, ((kw, pieces), piece_refs) in enumerate(zip(groups, a_refs)):
        wk = w[koff:koff + kw]
        koff += kw
        for a_ref, (r0, nr) in zip(piece_refs, pieces):
            def project(a_ref=a_ref, wk=wk, gi=gi):
                part = _dot(a_ref[...], wk)
                if gi == 0:
                    xo_ref[:, cols] = part
                else:
                    xo_ref[:, cols] += part

            if len(pieces) == 1:
                project()
            else:
                pl.when((i >= r0) & (i < r0 + nr))(project)

    @pl.when(j == nj - 1)
    def _():
        mc = mc_ref[0]
        gate_gain = mc[:, gate * D:(gate + 1) * D] * gp_ref[...]
        if has_next:
            mn = mn_ref[0]
            scale_gain = gn_ref[...] * (1.0 + mn[:, scale * D:(scale + 1) * D])
            shift_v = mn[:, shift * D:(shift + 1) * D]

        def rows_pass(r, carry):
            rows = pl.ds(pl.multiple_of(r * RES_ROWS, RES_ROWS), RES_ROWS)
            y = xo_ref[rows, :]
            if has_bias:
                y = y + b_ref[...]
            xn = x_ref[rows, :] + _rms(y) * gate_gain
            xo_ref[rows, :] = xn
            if has_next:
                uo_ref[rows, :] = (_rms(xn) * scale_gain + shift_v).astype(BF16)
            return carry

        lax.fori_loop(0, xo_ref.shape[0] // RES_ROWS, rows_pass, 0)


def matmul_residual(groups, w, layer, bias, x, m, g_post, mods_cur, gate, n_lat, seq, tm, nxt=None):
    kdim = w.shape[1]
    resident = kdim * D * w.dtype.itemsize <= RES_RESIDENT_BYTES
    tn = D if resident else RES_TN
    nj = D // tn
    mrow = lambda i, j: (_mod_row(i, tm, n_lat, seq), 0, 0)
    row = lambda i, j: (i, 0)
    fixed = lambda i, j: (0, 0)

    def piece_spec(arr, r0, nr):
        return pl.BlockSpec((tm, arr.shape[1]), lambda i, j: (jnp.clip(i - r0, 0, nr - 1), 0))

    in_specs = [piece_spec(*pc) for g in groups for pc in g]
    w_mode = {"pipeline_mode": pl.Buffered(1)} if resident else {}
    in_specs.append(pl.BlockSpec((None, kdim, tn), lambda i, j: (layer, 0, j), **w_mode))
    args = [pc[0] for g in groups for pc in g] + [w]
    if bias is not None:
        in_specs.append(pl.BlockSpec((1, D), fixed))
        args.append(bias.reshape(1, D))
    in_specs += [pl.BlockSpec((tm, D), row),
                 pl.BlockSpec((1, D), fixed),
                 pl.BlockSpec((1, 1, 6 * D), mrow)]
    args += [x, g_post.reshape(1, D), mods_cur]
    out_specs = [pl.BlockSpec((tm, D), row)]
    out_shape = [jax.ShapeDtypeStruct((m, D), F32)]
    shift = scale = 0
    if nxt is not None:
        g_next, mods_next, shift, scale = nxt
        in_specs += [pl.BlockSpec((1, D), fixed), pl.BlockSpec((1, 1, 6 * D), mrow)]
        args += [g_next.reshape(1, D), mods_next]
        out_specs.append(pl.BlockSpec((tm, D), row))
        out_shape.append(jax.ShapeDtypeStruct((m, D), BF16))
    group_meta = tuple((g[0][0].shape[1], tuple(pc[1:] for pc in g)) for g in groups)
    assert sum(kw for kw, _ in group_meta) == kdim
    outs = pl.pallas_call(
        functools.partial(_mm_res_body, groups=group_meta, gate=gate, shift=shift, scale=scale,
                          nj=nj, has_bias=bias is not None, has_next=nxt is not None),
        grid=(m // tm, nj),
        in_specs=in_specs,
        out_specs=out_specs,
        out_shape=out_shape,
        compiler_params=_cp("parallel", "arbitrary"),
        name="proj_residual",
    )(*args)
    return (outs[0], outs[1]) if nxt is not None else (outs[0], None)


GATED_HALO = 16


def _mm_gated_body(*refs, conv, silu, lat_tiles, seq_lat, seq_ctx):
    if conv:
        x_ref, xp_ref, xn_ref, wa_ref, wg_ref, ca_ref, cg_ref, ba_ref, bg_ref, o_ref = refs
    else:
        x_ref, wa_ref, wg_ref, ba_ref, bg_ref, o_ref = refs
    x = x_ref[...]
    wa = wa_ref[...].astype(BF16)
    wg = wg_ref[...].astype(BF16)
    za = _dot(x, wa)
    zg = _dot(x, wg)
    if conv:
        i = pl.program_id(0)
        tm, tn = za.shape
        halo = jnp.concatenate([xp_ref[...], xn_ref[...]], axis=0)
        ha = _dot(halo, wa)
        hg = _dot(halo, wg)
        is_ctx = i >= lat_tiles
        seq_m1 = jnp.where(is_ctx, seq_ctx - 1, seq_lat - 1)
        starts_seq = ((i * tm) & seq_m1) == 0
        ends_seq = (((i + 1) * tm) & seq_m1) == 0
        sub = lax.broadcasted_iota(jnp.int32, (SUBLANES, tn), 0)
        top = sub == 0
        bottom = sub == SUBLANES - 1
        inner = range(seq_ctx, tm, seq_ctx)
        g = SUBLANES

        def conv3(z, h, cw):
            up = pltpu.roll(z, 1, 0)
            dn = pltpu.roll(z, tm - 1, 0)
            before = jnp.where(starts_seq, 0.0, h[GATED_HALO - 1:GATED_HALO])
            after = jnp.where(ends_seq, 0.0, h[GATED_HALO:GATED_HALO + 1])
            up_parts, dn_parts = [jnp.where(top, before, up[:g])], []
            up_lo, dn_lo = g, 0
            for r in inner:
                up_parts += [up[up_lo:r], jnp.where(jnp.logical_and(top, is_ctx), 0.0, up[r:r + g])]
                dn_parts += [dn[dn_lo:r - g], jnp.where(jnp.logical_and(bottom, is_ctx), 0.0, dn[r - g:r])]
                up_lo, dn_lo = r + g, r
            up_parts.append(up[up_lo:])
            dn_parts += [dn[dn_lo:tm - g], jnp.where(bottom, after, dn[tm - g:])]
            up = jnp.concatenate(up_parts, axis=0)
            dn = jnp.concatenate(dn_parts, axis=0)
            return cw[0:1] * up + cw[1:2] * z + cw[2:3] * dn

        za = conv3(za, ha, ca_ref[...])
        zg = conv3(zg, hg, cg_ref[...])
    za = za + ba_ref[...]
    zg = zg + bg_ref[...]
    sg = _sigmoid(zg)
    o_ref[...] = (za * (zg * sg if silu else sg)).astype(o_ref.dtype)


def matmul_gated(x, w, layer, conv_w, bias, *, m, tm, tn, n_lat, seq_lat, seq_ctx, silu, out_dtype, name):
    kdim = x.shape[1]
    n = w.shape[2] // 2
    nj = n // tn
    first = lambda i, j: (layer, 0, j)
    second = lambda i, j: (layer, 0, j + nj)
    bias = bias.reshape(bias.shape[0], 1, 2 * n)
    in_specs = [pl.BlockSpec((tm, kdim), lambda i, j: (i, 0))]
    args = [x]
    if conv_w is not None:
        hb = tm // GATED_HALO
        last_hb = x.shape[0] // GATED_HALO - 1
        in_specs += [pl.BlockSpec((GATED_HALO, kdim), lambda i, j: (jnp.maximum(i * hb - 1, 0), 0)),
                     pl.BlockSpec((GATED_HALO, kdim), lambda i, j: (jnp.minimum((i + 1) * hb, last_hb), 0))]
        args += [x, x]
    in_specs += [pl.BlockSpec((None, kdim, tn), first), pl.BlockSpec((None, kdim, tn), second)]
    args += [w, w]
    if conv_w is not None:
        in_specs += [pl.BlockSpec((None, conv_w.shape[1], tn), first),
                     pl.BlockSpec((None, conv_w.shape[1], tn), second)]
        args += [conv_w, conv_w]
    in_specs += [pl.BlockSpec((None, 1, tn), first), pl.BlockSpec((None, 1, tn), second)]
    args += [bias, bias]
    return pl.pallas_call(
        functools.partial(_mm_gated_body, conv=conv_w is not None, silu=silu, lat_tiles=n_lat // tm,
                          seq_lat=seq_lat, seq_ctx=seq_ctx),
        grid=(m // tm, nj),
        in_specs=in_specs,
        out_specs=pl.BlockSpec((tm, tn), lambda i, j: (i, j)),
        out_shape=jax.ShapeDtypeStruct((m, n), out_dtype),
        compiler_params=_cp("parallel", "parallel"),
        name=name,
    )(*args)


def _dwconv_ln_body(prev_ref, cur_ref, next_ref, w_ref, b_ref, g_ref, beta_ref, o_ref, ext_ref, sh_ref, acc_ref,
                    *, tiles_per_seq, n_lat_tiles):
    i = pl.program_id(0)
    tm = cur_ref.shape[0]
    hw = CONV_HALO
    is_lat = i < n_lat_tiles
    j = i % tiles_per_seq
    has_prev = jnp.logical_and(is_lat, j > 0).astype(F32)
    has_next = jnp.logical_and(is_lat, j < tiles_per_seq - 1).astype(F32)
    ext_ref[0:hw, :] = prev_ref[tm - hw:, :] * has_prev
    ext_ref[hw:hw + tm, :] = cur_ref[...]
    ext_ref[hw + tm:, :] = next_ref[0:hw, :] * has_next

    sh_rows = sh_ref.shape[1]

    def lane_block(cb, carry):
        col = pl.multiple_of(cb * DH, DH)
        wv = w_ref[:, pl.ds(col, DH)]
        for s in range(SUBLANES):
            sh_ref[s] = ext_ref[s:s + sh_rows, pl.ds(col, DH)]
        for r0 in range(0, tm, 128):
            acc = jnp.zeros((128, DH), F32)
            for k in range(CONV_W):
                off = r0 + k + hw - (CONV_W - 1) // 2
                s = off % SUBLANES
                acc = acc + sh_ref[s, off - s:off - s + 128, :] * wv[k:k + 1]
            acc_ref[r0:r0 + 128, pl.ds(col, DH)] = acc
        return carry

    lax.fori_loop(0, D // DH, lane_block, 0)
    hcv = acc_ref[...] + b_ref[...]
    mu = jnp.mean(hcv, axis=-1, keepdims=True)
    xc = hcv - mu
    var = jnp.mean(xc * xc, axis=-1, keepdims=True)
    y = xc * lax.rsqrt(var + EPS) * g_ref[...] + beta_ref[...]
    o_ref[...] = (y * _sigmoid(y)).astype(BF16)


def dwconv_ln(hx, w_dw, b_dw, ln_g, ln_b, n_lat, seq):
    m = hx.shape[0]
    tm = 256
    nt = m // tm
    fixed = lambda i: (0, 0)
    vec = lambda a: a.reshape(1, D)
    return pl.pallas_call(
        functools.partial(_dwconv_ln_body, tiles_per_seq=seq // tm, n_lat_tiles=n_lat // tm),
        grid=(nt,),
        in_specs=[pl.BlockSpec((tm, D), lambda i: (jnp.maximum(i - 1, 0), 0)),
                  pl.BlockSpec((tm, D), lambda i: (i, 0)),
                  pl.BlockSpec((tm, D), lambda i: (jnp.minimum(i + 1, nt - 1), 0)),
                  pl.BlockSpec((CONV_W, D), fixed),
                  pl.BlockSpec((1, D), fixed),
                  pl.BlockSpec((1, D), fixed),
                  pl.BlockSpec((1, D), fixed)],
        out_specs=pl.BlockSpec((tm, D), lambda i: (i, 0)),
        out_shape=jax.ShapeDtypeStruct((m, D), BF16),
        scratch_shapes=[pltpu.VMEM((tm + 2 * CONV_HALO, D), F32),
                        pltpu.VMEM((SUBLANES, tm + 2 * CONV_HALO - SUBLANES, DH), F32),
                        pltpu.VMEM((tm, D), F32)],
        compiler_params=_cp("parallel"),
        name="dwconv_ln",
    )(hx, hx, hx, w_dw, vec(b_dw), vec(ln_g), vec(ln_b))


def _rope_tables(b, t, tc):
    rows = t // GRID_W
    row = jnp.repeat(jnp.arange(rows), GRID_W).astype(F32)
    col = jnp.tile(jnp.arange(GRID_W), rows).astype(F32)
    pairs = ROPE // 4
    freqs = ROPE_BASE ** (-jnp.arange(pairs, dtype=F32) / pairs)
    ar = row[:, None] * freqs
    ac = col[:, None] * freqs
    cos = jnp.concatenate([jnp.cos(ar), jnp.cos(ar), jnp.cos(ac), jnp.cos(ac), jnp.ones((t, DH - ROPE), F32)], axis=1)
    sin = jnp.concatenate([-jnp.sin(ar), jnp.sin(ar), -jnp.sin(ac), jnp.sin(ac), jnp.zeros((t, DH - ROPE), F32)], axis=1)
    cos = jnp.concatenate([jnp.tile(cos, (b, 1)), jnp.ones((b * tc, DH), F32)], axis=0)
    sin = jnp.concatenate([jnp.tile(sin, (b, 1)), jnp.zeros((b * tc, DH), F32)], axis=0)
    return cos, sin


def _pick(n, *cands):
    for cand in cands:
        if n % cand == 0:
            return cand
    raise ValueError(f"no tile for {n}")


def kernel(x, c, ctx, c_ctx, w_mod, b_mod, norm_gains, w_in_ab, mla_q_norm, w_q_up, mla_kv_norm, w_kv_up, hgrn_lb, hgrn_norm, w_out_ab, conv_w_pw1, conv_b_pw1, conv_w_dw, conv_b_dw, conv_ln_g, conv_ln_b, conv_w_pw2, conv_b_pw2, ffn_w_up, ffn_w_conv, ffn_b_conv, ffn_w_down):
    b, t, _ = x.shape
    tc = ctx.shape[1]
    depth = w_mod.shape[0]
    n_lat = b * t
    n_ctx = b * tc
    assert x.shape[2] == D and depth % 2 == 0 and b < 8
    assert t % 512 == 0 and tc % 256 == 0 and n_ctx % 512 == 0 and t % n_ctx == 0
    assert t & (t - 1) == 0 and tc & (tc - 1) == 0
    tg = _pick(n_ctx, 1024, 512)
    tr = RES_TM
    lat_tiles = n_lat // tr
    ctx_tiles = n_ctx // tr
    w_out_bf, w_pw2_bf, w_down_bf = w_out_ab.astype(BF16), conv_w_pw2.astype(BF16), ffn_w_down.astype(BF16)
    w_in_p = repack_in_proj(w_in_ab)

    hcur = jnp.concatenate([x.reshape(n_lat, D), ctx.reshape(n_ctx, D)], axis=0)
    cc = jnp.concatenate([c, c_ctx[None, :], jnp.zeros((7 - b, D), F32)], axis=0)
    mods = modulation(cc, w_mod, b_mod).reshape(depth, 8, 1, 6 * D)
    cos, sin = _rope_tables(b, t, tc)
    masks_np, lvl_np = _gla_tables()
    masks = jnp.asarray(masks_np, BF16)
    lvl = jnp.asarray(lvl_np)

    u = norm_mod(hcur, norm_gains[0, 0], mods[0], n_lat, t, shift=0, scale=1)
    for l in range(depth):
        last = l == depth - 1
        j = l // 2
        m = n_lat if last else n_lat + n_ctx
        m_out = n_lat if l >= depth - 2 else n_lat + n_ctx
        row_tiles = m_out // tr
        g_pre1, g_post1, g_pre2, g_post2 = norm_gains[l]
        if l % 2 == 0:
            wq = jnp.pad(w_q_up[j].reshape(QL, H, DH + ROPE), ((0, 0), (0, 0), (0, DH - ROPE)))
            wq = wq.reshape(QL, 2 * H * DH).astype(BF16)
            wkv = w_kv_up[j].reshape(KVL, H, 2 * DH)
            wkv = jnp.concatenate([wkv[:, :, :DH].reshape(KVL, H * DH), wkv[:, :, DH:].reshape(KVL, H * DH)], axis=1)
            p = matmul(u, w_in_p, j, _pick(m, 1024, 512), 512)
            q, k, v = mla_prep(p, mla_q_norm[j], wq, mla_kv_norm[j], wkv.astype(BF16), cos, sin)
            att_lat = attention_latent(q, k, v, b, t, tc)
            o_f, o_b = gla(p, hgrn_lb, masks, lvl, b, t, tc, j)
            hg = hg_merge(o_f, o_b, p, hgrn_norm[j])
            att_pieces = [(att_lat, 0, lat_tiles)]
            if m_out > n_lat:
                att_pieces.append((attention_context(q, k, v, b, t, tc), lat_tiles, ctx_tiles))
            groups = [att_pieces, [(hg, 0, row_tiles)]]
            w_o, b_o = w_out_bf, None
        else:
            glu = matmul_gated(u, conv_w_pw1, j, None, conv_b_pw1, m=m, tm=tg, tn=512, n_lat=n_lat, seq_lat=t,
                               seq_ctx=tc, silu=False, out_dtype=F32, name="conformer_glu")
            mix = dwconv_ln(glu, conv_w_dw[j], conv_b_dw[j], conv_ln_g[j], conv_ln_b[j], n_lat, t)
            groups = [[(mix, 0, row_tiles)]]
            w_o, b_o = w_pw2_bf, conv_b_pw2[j]
        hcur, u = matmul_residual(groups, w_o, j, b_o, hcur, m_out, g_post1, mods[l], 2, n_lat, t, tr,
                                  nxt=(g_pre2, mods[l], 3, 4))
        act = matmul_gated(u, ffn_w_up, l, ffn_w_conv, ffn_b_conv, m=m_out, tm=tg, tn=512, n_lat=n_lat, seq_lat=t,
                           seq_ctx=tc, silu=True, out_dtype=BF16, name="ffn_up")
        nxt = None if last else (norm_gains[l + 1, 0], mods[l + 1], 0, 1)
        hcur, u = matmul_residual([[(act, 0, row_tiles)]], w_down_bf, l, None, hcur, m_out, g_post2, mods[l], 5,
                                  n_lat, t, tr, nxt=nxt)
    return hcur.reshape(b, t, D)
```

```python
import functools

import numpy as np
import jax
import jax.numpy as jnp
from jax import lax
from jax.experimental import pallas as pl
from jax.experimental.pallas import tpu as pltpu

F32 = jnp.float32
BF16 = jnp.bfloat16

D = 2048
H = D // 256
DH = 128
ROPE = 64
QL = D // 4
KVL = D // 8
DFF = (11 * D) // 4
GRID_W = 64
ROPE_BASE = 10000.0
CONV_W = 31
CONV_HALO = 16
EPS = 1e-6
ATTN_SCALE = (DH + ROPE) ** -0.5
LOG2_E = float(np.log2(np.e))
Q_SCALE = ATTN_SCALE * LOG2_E
ATTN_HB = 4
P_FRONT = 1024
P_WIDTH = P_FRONT + 5 * H * DH
GLA_C = 128
GLA_LEVELS = 7
GLA_HB = 8
SUBLANES = 8
VMEM_LIMIT = 56 * 1024 * 1024


def _cp(*sem):
    return pltpu.CompilerParams(dimension_semantics=sem, vmem_limit_bytes=VMEM_LIMIT)


def _sigmoid(x):
    return 1.0 / (1.0 + jnp.exp(-x))


def _rms(x):
    return x * lax.rsqrt(jnp.mean(x * x, axis=-1, keepdims=True) + EPS)


def _dot(a, b):
    return jnp.dot(a, b, preferred_element_type=F32)


def _dot_nt(a, b):
    return lax.dot_general(a, b, (((1,), (1,)), ((), ())), preferred_element_type=F32)


def _dot_tn(a, b):
    return lax.dot_general(a, b, (((0,), (0,)), ((), ())), preferred_element_type=F32)


def _mod_row(i, tm, n_lat, seq):
    return jnp.where(i < n_lat // tm, i // (seq // tm), n_lat // seq)


def _mod_body(c_ref, w_ref, b_ref, o_ref):
    c = c_ref[...]
    s = (c * _sigmoid(c)).astype(BF16)
    o_ref[0] = _dot(s, w_ref[0].astype(BF16)) + b_ref[0]


def modulation(cc, w_mod, b_mod):
    depth, _, n = w_mod.shape
    rows = cc.shape[0]
    tn = 1024
    return pl.pallas_call(
        _mod_body,
        grid=(depth, n // tn),
        in_specs=[pl.BlockSpec((rows, D), lambda l, j: (0, 0)),
                  pl.BlockSpec((1, D, tn), lambda l, j: (l, 0, j)),
                  pl.BlockSpec((1, 1, tn), lambda l, j: (l, 0, j))],
        out_specs=pl.BlockSpec((1, rows, tn), lambda l, j: (l, 0, j)),
        out_shape=jax.ShapeDtypeStruct((depth, rows, n), F32),
        compiler_params=_cp("parallel", "parallel"),
        name="modulation",
    )(cc, w_mod, b_mod.reshape(depth, 1, n))


def _norm_mod_body(xl_ref, xc_ref, g_ref, m_ref, o_ref, h_ref, *, shift, scale, lat_tiles):
    m = m_ref[0]
    x = jnp.where(pl.program_id(0) < lat_tiles, xl_ref[...], xc_ref[...])
    h_ref[...] = x
    y = _rms(x) * g_ref[...]
    o_ref[...] = (y * (1.0 + m[:, scale * D:(scale + 1) * D]) + m[:, shift * D:(shift + 1) * D]).astype(BF16)


def norm_mod(x_lat, x_ctx, gain, mods, seq, shift, scale):
    n_lat, n_ctx = x_lat.shape[0], x_ctx.shape[0]
    m = n_lat + n_ctx
    tm = 256
    lat_tiles = n_lat // tm
    row = lambda i: (i, 0)
    return pl.pallas_call(
        functools.partial(_norm_mod_body, shift=shift, scale=scale, lat_tiles=lat_tiles),
        grid=(m // tm,),
        in_specs=[pl.BlockSpec((tm, D), lambda i: (jnp.minimum(i, lat_tiles - 1), 0)),
                  pl.BlockSpec((tm, D), lambda i: (jnp.maximum(i - lat_tiles, 0), 0)),
                  pl.BlockSpec((1, D), lambda i: (0, 0)),
                  pl.BlockSpec((1, 1, 6 * D), lambda i: (_mod_row(i, tm, n_lat, seq), 0, 0))],
        out_specs=[pl.BlockSpec((tm, D), row), pl.BlockSpec((tm, D), row)],
        out_shape=[jax.ShapeDtypeStruct((m, D), BF16), jax.ShapeDtypeStruct((m, D), F32)],
        compiler_params=_cp("parallel"),
        name="norm_mod",
    )(x_lat, x_ctx, gain.reshape(1, D), mods)


def _repack_in_body(w_ref, o_ref):
    front = QL + KVL + ROPE
    w = w_ref[...]
    o_ref[:, :front] = w[:, :front].astype(BF16)
    o_ref[:, front:P_FRONT] = jnp.zeros((w.shape[0], P_FRONT - front), BF16)
    o_ref[:, P_FRONT:] = w[:, front:].astype(BF16)


def repack_in_proj(w_in):
    layers, k, n = w_in.shape
    tr = 256
    return pl.pallas_call(
        _repack_in_body,
        grid=(layers, k // tr),
        in_specs=[pl.BlockSpec((None, tr, n), lambda l, i: (l, i, 0))],
        out_specs=pl.BlockSpec((None, tr, P_WIDTH), lambda l, i: (l, i, 0)),
        out_shape=jax.ShapeDtypeStruct((layers, k, P_WIDTH), BF16),
        compiler_params=_cp("parallel", "parallel"),
        name="repack_in_proj",
    )(w_in)


def _mm_body(x_ref, w_ref, o_ref):
    o_ref[...] = _dot(x_ref[...], w_ref[...])


def matmul(x, w, layer, tm, tn):
    m, k = x.shape
    n = w.shape[2]
    return pl.pallas_call(
        _mm_body,
        grid=(m // tm, n // tn),
        in_specs=[pl.BlockSpec((tm, k), lambda i, j: (i, 0)),
                  pl.BlockSpec((None, k, tn), lambda i, j: (layer, 0, j))],
        out_specs=pl.BlockSpec((tm, tn), lambda i, j: (i, j)),
        out_shape=jax.ShapeDtypeStruct((m, n), F32),
        compiler_params=_cp("parallel", "parallel"),
        name="in_proj",
    )(x, w)


def _mla_prep_body(p_ref, qn_ref, wq_ref, kvn_ref, wkv_ref, cos_ref, sin_ref, q_ref, k_ref, v_ref):
    p = p_ref[...]
    tm = p.shape[0]
    cos = cos_ref[...]
    sin = sin_ref[...]
    lane = lax.broadcasted_iota(jnp.int32, (tm, DH), 1)
    first_half = (lane & 31) < 16

    def rope(x):
        partner = jnp.where(first_half, pltpu.roll(x, DH - 16, 1), pltpu.roll(x, 16, 1))
        return x * cos + partner * sin

    cq = (_rms(p[:, :QL]) * qn_ref[...]).astype(BF16)
    ckv = (_rms(p[:, QL:QL + KVL]) * kvn_ref[...]).astype(BF16)
    kr = rope(p[:, QL + KVL:QL + KVL + DH]).astype(BF16)
    q = _dot(cq, wq_ref[...])
    kv = _dot(ckv, wkv_ref[...])
    for h in range(H):
        lo = 2 * DH * h
        q_ref[:, lo:lo + DH] = (q[:, lo:lo + DH] * Q_SCALE).astype(BF16)
        q_ref[:, lo + DH:lo + 2 * DH] = (rope(q[:, lo + DH:lo + 2 * DH]) * Q_SCALE).astype(BF16)
        k_ref[:, lo:lo + DH] = kv[:, DH * h:DH * (h + 1)].astype(BF16)
        k_ref[:, lo + DH:lo + 2 * DH] = kr
    v_ref[...] = kv[:, H * DH:].astype(BF16)


def mla_prep(p, q_norm, wq, kv_norm, wkv, cos, sin):
    m = p.shape[0]
    tm = 512
    row = lambda i: (i, 0)
    fixed = lambda i: (0, 0)
    return pl.pallas_call(
        _mla_prep_body,
        grid=(m // tm,),
        in_specs=[pl.BlockSpec((tm, P_FRONT), row),
                  pl.BlockSpec((1, QL), fixed),
                  pl.BlockSpec((QL, 2 * H * DH), fixed),
                  pl.BlockSpec((1, KVL), fixed),
                  pl.BlockSpec((KVL, 2 * H * DH), fixed),
                  pl.BlockSpec((tm, DH), row),
                  pl.BlockSpec((tm, DH), row)],
        out_specs=[pl.BlockSpec((tm, 2 * H * DH), row),
                   pl.BlockSpec((tm, 2 * H * DH), row),
                   pl.BlockSpec((tm, H * DH), row)],
        out_shape=[jax.ShapeDtypeStruct((m, 2 * H * DH), BF16),
                   jax.ShapeDtypeStruct((m, 2 * H * DH), BF16),
                   jax.ShapeDtypeStruct((m, H * DH), BF16)],
        compiler_params=_cp("parallel"),
        name="mla_prep",
    )(p, q_norm.reshape(1, QL), wq, kv_norm.reshape(1, KVL), wkv, cos, sin)


def _attn_lat_body(q_ref, kl_ref, vl_ref, kc_ref, vc_ref, o_ref):
    for hh in range(ATTN_HB):
        qk = slice(hh * 2 * DH, (hh + 1) * 2 * DH)
        vo = slice(hh * DH, (hh + 1) * DH)
        q = q_ref[:, qk]
        sl = _dot_nt(q, kl_ref[:, qk])
        sc = _dot_nt(q, kc_ref[:, qk])
        mx = jnp.maximum(jnp.max(sl, axis=-1, keepdims=True), jnp.max(sc, axis=-1, keepdims=True))
        el = jnp.exp2(sl - mx)
        ec = jnp.exp2(sc - mx)
        den = jnp.sum(el, axis=-1, keepdims=True) + jnp.sum(ec, axis=-1, keepdims=True)
        o = _dot(el.astype(BF16), vl_ref[:, vo]) + _dot(ec.astype(BF16), vc_ref[:, vo])
        o_ref[:, vo] = (o / den).astype(BF16)


def _attn_ctx_body(q_ref, kc_ref, vc_ref, o_ref):
    sc = _dot_nt(q_ref[...], kc_ref[...])
    ec = jnp.exp2(sc - jnp.max(sc, axis=-1, keepdims=True))
    o = _dot(ec.astype(BF16), vc_ref[...])
    o_ref[...] = (o / jnp.sum(ec, axis=-1, keepdims=True)).astype(BF16)


def attention_latent(q, k, v, b, t, tc):
    n_lat = b * t
    tq = 512
    nq = t // tq
    cb = n_lat // tc
    wqk = ATTN_HB * 2 * DH
    wv = ATTN_HB * DH
    return pl.pallas_call(
        _attn_lat_body,
        grid=(b, H // ATTN_HB, nq),
        in_specs=[pl.BlockSpec((tq, wqk), lambda bi, h, qi: (bi * nq + qi, h)),
                  pl.BlockSpec((t, wqk), lambda bi, h, qi: (bi, h)),
                  pl.BlockSpec((t, wv), lambda bi, h, qi: (bi, h)),
                  pl.BlockSpec((tc, wqk), lambda bi, h, qi: (cb + bi, h)),
                  pl.BlockSpec((tc, wv), lambda bi, h, qi: (cb + bi, h))],
        out_specs=pl.BlockSpec((tq, wv), lambda bi, h, qi: (bi * nq + qi, h)),
        out_shape=jax.ShapeDtypeStruct((n_lat, H * DH), BF16),
        compiler_params=_cp("parallel", "parallel", "parallel"),
        name="attn_latent",
    )(q, k, v, k, v)


def attention_context(q, k, v, b, t, tc):
    cb = (b * t) // tc
    return pl.pallas_call(
        _attn_ctx_body,
        grid=(b, H),
        in_specs=[pl.BlockSpec((tc, 2 * DH), lambda bi, h: (cb + bi, h)),
                  pl.BlockSpec((tc, 2 * DH), lambda bi, h: (cb + bi, h)),
                  pl.BlockSpec((tc, DH), lambda bi, h: (cb + bi, h))],
        out_specs=pl.BlockSpec((tc, DH), lambda bi, h: (bi, h)),
        out_shape=jax.ShapeDtypeStruct((b * tc, H * DH), BF16),
        compiler_params=_cp("parallel", "parallel"),
        name="attn_context",
    )(q, k, v)


def _gla_tables():
    c = GLA_C
    t = np.arange(c)[:, None]
    r = np.arange(c)[None, :]
    mats = []
    for lv in range(GLA_LEVELS):
        same = (t >> (lv + 1)) == (r >> (lv + 1))
        bt = (t >> lv) & 1
        br = (r >> lv) & 1
        upper = same & (bt == 1) & (br == 1) & (r <= t)
        lower = same & (bt == 0) & (br == 0) & (r > t)
        mats.append(upper | lower)
    mats.append(r <= t)
    mats.append(r > t)
    fwd = np.stack(mats).astype(np.float32)
    bwd = fwd[:, ::-1, ::-1]
    masks = np.stack([fwd, bwd]).reshape(2, (GLA_LEVELS + 2) * c, c)
    masks = np.concatenate([masks, masks], axis=2)
    x = t ^ r
    lvl = np.where(x == 0, -1, np.floor(np.log2(np.maximum(x, 1)))).astype(np.int32)
    return masks, lvl


def _gla_gates(x, lb_raw, layer):
    log_sig = jnp.minimum(x, 0.0) - jnp.log1p(jnp.exp(-jnp.abs(x)))
    if layer == 0:
        g = log_sig
        kk = 1.0 / (1.0 + jnp.exp(x))
    else:
        e = jnp.exp(lb_raw - jnp.max(lb_raw, axis=0, keepdims=True))
        prob = e / jnp.sum(e, axis=0, keepdims=True)
        lb = jnp.sum(prob[1:layer + 1], axis=0, keepdims=True)
        a = jnp.log(lb)
        bb = jnp.log1p(-lb) + log_sig
        g = jnp.maximum(a, bb) + jnp.log1p(jnp.exp(-jnp.abs(a - bb)))
        kk = (1.0 - lb) / (1.0 + jnp.exp(x))
    return g, kk


def _gla_chain(q, kk, v, ex, decay, level_is, row_bit, st_ref, slot, *, d):
    c = GLA_C
    v = v.astype(BF16)
    att = jnp.where(level_is[-1], _dot_nt(q.astype(BF16), kk.astype(BF16)), 0.0)
    for lv in range(GLA_LEVELS):
        ex_l = ex[lv * c:(lv + 1) * c]
        half = 1 << lv
        if half < SUBLANES:
            is_q = row_bit[lv] if d == 0 else jnp.logical_not(row_bit[lv])
            scaled = jnp.where(is_q, q, kk) * ex_l
            ql = jnp.where(is_q, scaled, 0.0)
            kl = jnp.where(is_q, 0.0, scaled)
            att = jnp.where(level_is[lv], _dot_nt(ql.astype(BF16), kl.astype(BF16)), att)
        else:
            zeros = jnp.zeros((half, DH), F32)
            blocks = [slice(blk * half, (blk + 1) * half) for blk in range(c // half)]
            q_side = [(blk & 1) == 1 - d for blk in range(c // half)]
            ql = jnp.concatenate([q[rows] * ex_l[rows] for rows, qs in zip(blocks, q_side) if qs], axis=0)
            kl = jnp.concatenate([zeros if qs else kk[rows] * ex_l[rows] for rows, qs in zip(blocks, q_side)], axis=0)
            prod = _dot_nt(ql.astype(BF16), kl.astype(BF16))
            parts, used = [], 0
            for rows, qs in zip(blocks, q_side):
                if qs:
                    parts.append(jnp.where(level_is[lv][rows], prod[used:used + half], att[rows]))
                    used += half
                else:
                    parts.append(att[rows])
            att = jnp.concatenate(parts, axis=0)

    q_in = (q * ex[GLA_LEVELS * c:(GLA_LEVELS + 1) * c]).astype(BF16)
    k_out = (kk * ex[(GLA_LEVELS + 1) * c:]).astype(BF16)
    st = st_ref[slot]
    o = _dot(att.astype(BF16), v) + _dot_nt(q_in, st.astype(BF16))
    st_ref[slot] = st * decay + _dot_tn(v, k_out)
    return o


def _gla_body(qf_ref, ff_ref, vf_ref, qb_ref, fb_ref, vb_ref, lb_ref, msk_ref, lvl_ref, of_ref, ob_ref, st_ref,
              *, layer):
    c = GLA_C

    @pl.when(pl.program_id(2) == 0)
    def _():
        st_ref[...] = jnp.zeros_like(st_ref)

    lvl = lvl_ref[...]
    level_is = [lvl == lv for lv in range(GLA_LEVELS)] + [lvl == -1]
    row = lax.broadcasted_iota(jnp.int32, (c, DH), 0)
    row_bit = [((row >> lv) & 1) == 1 for lv in range(GLA_LEVELS)]
    for d, (q_ref, f_ref, v_ref, o_ref) in enumerate(((qf_ref, ff_ref, vf_ref, of_ref),
                                                      (qb_ref, fb_ref, vb_ref, ob_ref))):
        g, kk = _gla_gates(f_ref[...], lb_ref[d], layer)
        g = g * LOG2_E
        g_hi = g.astype(BF16)
        g_lo = (g - g_hi.astype(F32)).astype(BF16)
        ex = jnp.exp2(_dot(msk_ref[d], jnp.concatenate([g_hi, g_lo], axis=0)))
        decay = jnp.exp2(jnp.sum(g, axis=0, keepdims=True))
        for hh in range(GLA_HB):
            cols = slice(hh * DH, (hh + 1) * DH)
            o_ref[:, cols] = _gla_chain(q_ref[:, cols], kk[:, cols], v_ref[:, cols], ex[:, cols], decay[:, cols],
                                        level_is, row_bit, st_ref, d * GLA_HB + hh, d=d)


def gla(p, lb_raw, masks, lvl, b, t, tc, layer):
    c = GLA_C
    m = p.shape[0]
    n_ctx_chunks = tc // c
    n_lat_chunks = t // c
    n_chunks = n_ctx_chunks + n_lat_chunks
    ctx0 = (b * t) // c
    n_even = lb_raw.shape[1]
    wb = GLA_HB * DH

    def blk(bi, i, d):
        jc = i if d == 0 else n_ctx_chunks - 1 - i
        jl = i - n_ctx_chunks if d == 0 else n_chunks - 1 - i
        return jnp.where(i < n_ctx_chunks, ctx0 + bi * n_ctx_chunks + jc, bi * n_lat_chunks + jl)

    def spec(section, d):
        col0 = (P_FRONT + section * H * DH) // wb
        return pl.BlockSpec((c, wb), lambda bi, hh, i: (blk(bi, i, d), col0 + hh))

    out_spec = lambda d: pl.BlockSpec((c, wb), lambda bi, hh, i: (blk(bi, i, d), hh))
    return pl.pallas_call(
        functools.partial(_gla_body, layer=layer),
        grid=(b, H // GLA_HB, n_chunks),
        in_specs=[spec(0, 0), spec(1, 0), spec(3, 0), spec(0, 1), spec(2, 1), spec(3, 1),
                  pl.BlockSpec((2, n_even, wb), lambda bi, hh, i: (0, 0, hh)),
                  pl.BlockSpec((2, (GLA_LEVELS + 2) * c, 2 * c), lambda bi, hh, i: (0, 0, 0)),
                  pl.BlockSpec((c, c), lambda bi, hh, i: (0, 0))],
        out_specs=[out_spec(0), out_spec(1)],
        out_shape=[jax.ShapeDtypeStruct((m, H * DH), F32)] * 2,
        scratch_shapes=[pltpu.VMEM((2 * GLA_HB, DH, DH), F32)],
        compiler_params=_cp("parallel", "parallel", "arbitrary"),
        name="hgrn_scan",
    )(p, p, p, p, p, p, lb_raw, masks, lvl)


def _hg_merge_body(of_ref, ob_ref, g_ref, n_ref, y_ref):
    o = of_ref[...] + ob_ref[...]
    gate = g_ref[...]
    gain = n_ref[...]
    for h in range(H):
        sl = slice(h * DH, (h + 1) * DH)
        gh = gate[:, sl]
        y_ref[:, sl] = (_rms(o[:, sl]) * gain * (gh * _sigmoid(gh))).astype(BF16)


def hg_merge(o_f, o_b, p, hg_norm):
    m = p.shape[0]
    tm = 512
    w = H * DH
    row = lambda i: (i, 0)
    return pl.pallas_call(
        _hg_merge_body,
        grid=(m // tm,),
        in_specs=[pl.BlockSpec((tm, w), row),
                  pl.BlockSpec((tm, w), row),
                  pl.BlockSpec((tm, w), lambda i: (i, (P_WIDTH - w) // w)),
                  pl.BlockSpec((1, DH), lambda i: (0, 0))],
        out_specs=pl.BlockSpec((tm, w), row),
        out_shape=jax.ShapeDtypeStruct((m, w), BF16),
        compiler_params=_cp("parallel"),
        name="hgrn_merge",
    )(o_f, o_b, p, hg_norm.reshape(1, DH))


RES_TM = 512
RES_TN = 512
RES_ROWS = 128
RES_RESIDENT_BYTES = 8 * 1024 * 1024


def _mm_res_body(*refs, groups, gate, shift, scale, nj, has_bias, has_next):
    refs = list(refs)
    a_refs = [[refs.pop(0) for _ in pieces] for _, pieces in groups]
    w_ref = refs.pop(0)
    b_ref = refs.pop(0) if has_bias else None
    x_ref, gp_ref, mc_ref = refs.pop(0), refs.pop(0), refs.pop(0)
    gn_ref, mn_ref = (refs.pop(0), refs.pop(0)) if has_next else (None, None)
    xo_ref = refs.pop(0)
    uo_ref = refs.pop(0) if has_next else None
    i = pl.program_id(0)
    j = pl.program_id(1)
    tn = w_ref.shape[1]
    cols = pl.ds(pl.multiple_of(j * tn, tn), tn)

    w = w_ref[...]
    koff = 0
    for gi, ((kw, pieces), piece_refs) in enumerate(zip(groups, a_refs)):
        wk = w[koff:koff + kw]
        koff += kw
        for a_ref, (r0, nr) in zip(piece_refs, pieces):
            def project(a_ref=a_ref, wk=wk, gi=gi):
                part = _dot(a_ref[...], wk)
                if gi == 0:
                    xo_ref[:, cols] = part
                else:
                    xo_ref[:, cols] += part

            if len(pieces) == 1:
                project()
            else:
                pl.when((i >= r0) & (i < r0 + nr))(project)

    @pl.when(j == nj - 1)
    def _():
        mc = mc_ref[0]
        gate_gain = mc[:, gate * D:(gate + 1) * D] * gp_ref[...]
        if has_next:
            mn = mn_ref[0]
            scale_gain = gn_ref[...] * (1.0 + mn[:, scale * D:(scale + 1) * D])
            shift_v = mn[:, shift * D:(shift + 1) * D]

        def rows_pass(r, carry):
            rows = pl.ds(pl.multiple_of(r * RES_ROWS, RES_ROWS), RES_ROWS)
            y = xo_ref[rows, :]
            if has_bias:
                y = y + b_ref[...]
            xn = x_ref[rows, :] + _rms(y) * gate_gain
            xo_ref[rows, :] = xn
            if has_next:
                uo_ref[rows, :] = (_rms(xn) * scale_gain + shift_v).astype(BF16)
            return carry

        lax.fori_loop(0, xo_ref.shape[0] // RES_ROWS, rows_pass, 0)


def matmul_residual(groups, w, layer, bias, x, m, g_post, mods_cur, gate, n_lat, seq, tm, nxt=None):
    kdim = w.shape[1]
    resident = kdim * D * w.dtype.itemsize <= RES_RESIDENT_BYTES
    tn = D if resident else RES_TN
    nj = D // tn
    mrow = lambda i, j: (_mod_row(i, tm, n_lat, seq), 0, 0)
    row = lambda i, j: (i, 0)
    fixed = lambda i, j: (0, 0)

    def piece_spec(arr, r0, nr):
        return pl.BlockSpec((tm, arr.shape[1]), lambda i, j: (jnp.clip(i - r0, 0, nr - 1), 0))

    in_specs = [piece_spec(*pc) for g in groups for pc in g]
    w_mode = {"pipeline_mode": pl.Buffered(1)} if resident else {}
    in_specs.append(pl.BlockSpec((None, kdim, tn), lambda i, j: (layer, 0, j), **w_mode))
    args = [pc[0] for g in groups for pc in g] + [w]
    if bias is not None:
        in_specs.append(pl.BlockSpec((1, D), fixed))
        args.append(bias.reshape(1, D))
    in_specs += [pl.BlockSpec((tm, D), row),
                 pl.BlockSpec((1, D), fixed),
                 pl.BlockSpec((1, 1, 6 * D), mrow)]
    args += [x, g_post.reshape(1, D), mods_cur]
    out_specs = [pl.BlockSpec((tm, D), row)]
    out_shape = [jax.ShapeDtypeStruct((m, D), F32)]
    shift = scale = 0
    if nxt is not None:
        g_next, mods_next, shift, scale = nxt
        in_specs += [pl.BlockSpec((1, D), fixed), pl.BlockSpec((1, 1, 6 * D), mrow)]
        args += [g_next.reshape(1, D), mods_next]
        out_specs.append(pl.BlockSpec((tm, D), row))
        out_shape.append(jax.ShapeDtypeStruct((m, D), BF16))
    group_meta = tuple((g[0][0].shape[1], tuple(pc[1:] for pc in g)) for g in groups)
    assert sum(kw for kw, _ in group_meta) == kdim
    outs = pl.pallas_call(
        functools.partial(_mm_res_body, groups=group_meta, gate=gate, shift=shift, scale=scale,
                          nj=nj, has_bias=bias is not None, has_next=nxt is not None),
        grid=(m // tm, nj),
        in_specs=in_specs,
        out_specs=out_specs,
        out_shape=out_shape,
        compiler_params=_cp("parallel", "arbitrary"),
        name="proj_residual",
    )(*args)
    return (outs[0], outs[1]) if nxt is not None else (outs[0], None)


GATED_HALO = 16


def _mm_gated_body(*refs, conv, silu, lat_tiles, seq_lat, seq_ctx):
    if conv:
        x_ref, xp_ref, xn_ref, wa_ref, wg_ref, ca_ref, cg_ref, ba_ref, bg_ref, o_ref = refs
    else:
        x_ref, wa_ref, wg_ref, ba_ref, bg_ref, o_ref = refs
    x = x_ref[...]
    wa = wa_ref[...].astype(BF16)
    wg = wg_ref[...].astype(BF16)
    za = _dot(x, wa)
    zg = _dot(x, wg)
    if conv:
        i = pl.program_id(0)
        tm, tn = za.shape
        halo = jnp.concatenate([xp_ref[...], xn_ref[...]], axis=0)
        ha = _dot(halo, wa)
        hg = _dot(halo, wg)
        is_ctx = i >= lat_tiles
        seq_m1 = jnp.where(is_ctx, seq_ctx - 1, seq_lat - 1)
        starts_seq = ((i * tm) & seq_m1) == 0
        ends_seq = (((i + 1) * tm) & seq_m1) == 0
        sub = lax.broadcasted_iota(jnp.int32, (SUBLANES, tn), 0)
        top = sub == 0
        bottom = sub == SUBLANES - 1
        inner = range(seq_ctx, tm, seq_ctx)
        g = SUBLANES

        def conv3(z, h, cw):
            up = pltpu.roll(z, 1, 0)
            dn = pltpu.roll(z, tm - 1, 0)
            before = jnp.where(starts_seq, 0.0, h[GATED_HALO - 1:GATED_HALO])
            after = jnp.where(ends_seq, 0.0, h[GATED_HALO:GATED_HALO + 1])
            up_parts, dn_parts = [jnp.where(top, before, up[:g])], []
            up_lo, dn_lo = g, 0
            for r in inner:
                up_parts += [up[up_lo:r], jnp.where(jnp.logical_and(top, is_ctx), 0.0, up[r:r + g])]
                dn_parts += [dn[dn_lo:r - g], jnp.where(jnp.logical_and(bottom, is_ctx), 0.0, dn[r - g:r])]
                up_lo, dn_lo = r + g, r
            up_parts.append(up[up_lo:])
            dn_parts += [dn[dn_lo:tm - g], jnp.where(bottom, after, dn[tm - g:])]
            up = jnp.concatenate(up_parts, axis=0)
            dn = jnp.concatenate(dn_parts, axis=0)
            return cw[0:1] * up + cw[1:2] * z + cw[2:3] * dn

        za = conv3(za, ha, ca_ref[...])
        zg = conv3(zg, hg, cg_ref[...])
    za = za + ba_ref[...]
    zg = zg + bg_ref[...]
    sg = _sigmoid(zg)
    o_ref[...] = (za * (zg * sg if silu else sg)).astype(o_ref.dtype)


def matmul_gated(x, w, layer, conv_w, bias, *, m, tm, tn, n_lat, seq_lat, seq_ctx, silu, out_dtype, name):
    kdim = x.shape[1]
    n = w.shape[2] // 2
    nj = n // tn
    first = lambda i, j: (layer, 0, j)
    second = lambda i, j: (layer, 0, j + nj)
    bias = bias.reshape(bias.shape[0], 1, 2 * n)
    in_specs = [pl.BlockSpec((tm, kdim), lambda i, j: (i, 0))]
    args = [x]
    if conv_w is not None:
        hb = tm // GATED_HALO
        last_hb = x.shape[0] // GATED_HALO - 1
        in_specs += [pl.BlockSpec((GATED_HALO, kdim), lambda i, j: (jnp.maximum(i * hb - 1, 0), 0)),
                     pl.BlockSpec((GATED_HALO, kdim), lambda i, j: (jnp.minimum((i + 1) * hb, last_hb), 0))]
        args += [x, x]
    in_specs += [pl.BlockSpec((None, kdim, tn), first), pl.BlockSpec((None, kdim, tn), second)]
    args += [w, w]
    if conv_w is not None:
        in_specs += [pl.BlockSpec((None, conv_w.shape[1], tn), first),
                     pl.BlockSpec((None, conv_w.shape[1], tn), second)]
        args += [conv_w, conv_w]
    in_specs += [pl.BlockSpec((None, 1, tn), first), pl.BlockSpec((None, 1, tn), second)]
    args += [bias, bias]
    return pl.pallas_call(
        functools.partial(_mm_gated_body, conv=conv_w is not None, silu=silu, lat_tiles=n_lat // tm,
                          seq_lat=seq_lat, seq_ctx=seq_ctx),
        grid=(m // tm, nj),
        in_specs=in_specs,
        out_specs=pl.BlockSpec((tm, tn), lambda i, j: (i, j)),
        out_shape=jax.ShapeDtypeStruct((m, n), out_dtype),
        compiler_params=_cp("parallel", "parallel"),
        name=name,
    )(*args)


def _dwconv_ln_body(prev_ref, cur_ref, next_ref, w_ref, b_ref, g_ref, beta_ref, o_ref, ext_ref, sh_ref, acc_ref,
                    *, tiles_per_seq, n_lat_tiles):
    i = pl.program_id(0)
    tm = cur_ref.shape[0]
    hw = CONV_HALO
    is_lat = i < n_lat_tiles
    j = i % tiles_per_seq
    has_prev = jnp.logical_and(is_lat, j > 0).astype(F32)
    has_next = jnp.logical_and(is_lat, j < tiles_per_seq - 1).astype(F32)
    ext_ref[0:hw, :] = prev_ref[tm - hw:, :] * has_prev
    ext_ref[hw:hw + tm, :] = cur_ref[...]
    ext_ref[hw + tm:, :] = next_ref[0:hw, :] * has_next

    sh_rows = sh_ref.shape[1]

    def lane_block(cb, carry):
        col = pl.multiple_of(cb * DH, DH)
        wv = w_ref[:, pl.ds(col, DH)]
        for s in range(SUBLANES):
            sh_ref[s] = ext_ref[s:s + sh_rows, pl.ds(col, DH)]
        for r0 in range(0, tm, 128):
            acc = jnp.zeros((128, DH), F32)
            for k in range(CONV_W):
                off = r0 + k + hw - (CONV_W - 1) // 2
                s = off % SUBLANES
                acc = acc + sh_ref[s, off - s:off - s + 128, :] * wv[k:k + 1]
            acc_ref[r0:r0 + 128, pl.ds(col, DH)] = acc
        return carry

    lax.fori_loop(0, D // DH, lane_block, 0)
    hcv = acc_ref[...] + b_ref[...]
    mu = jnp.mean(hcv, axis=-1, keepdims=True)
    xc = hcv - mu
    var = jnp.mean(xc * xc, axis=-1, keepdims=True)
    y = xc * lax.rsqrt(var + EPS) * g_ref[...] + beta_ref[...]
    o_ref[...] = (y * _sigmoid(y)).astype(BF16)


def dwconv_ln(hx, w_dw, b_dw, ln_g, ln_b, n_lat, seq):
    m = hx.shape[0]
    tm = 256
    nt = m // tm
    fixed = lambda i: (0, 0)
    vec = lambda a: a.reshape(1, D)
    return pl.pallas_call(
        functools.partial(_dwconv_ln_body, tiles_per_seq=seq // tm, n_lat_tiles=n_lat // tm),
        grid=(nt,),
        in_specs=[pl.BlockSpec((tm, D), lambda i: (jnp.maximum(i - 1, 0), 0)),
                  pl.BlockSpec((tm, D), lambda i: (i, 0)),
                  pl.BlockSpec((tm, D), lambda i: (jnp.minimum(i + 1, nt - 1), 0)),
                  pl.BlockSpec((CONV_W, D), fixed),
                  pl.BlockSpec((1, D), fixed),
                  pl.BlockSpec((1, D), fixed),
                  pl.BlockSpec((1, D), fixed)],
        out_specs=pl.BlockSpec((tm, D), lambda i: (i, 0)),
        out_shape=jax.ShapeDtypeStruct((m, D), BF16),
        scratch_shapes=[pltpu.VMEM((tm + 2 * CONV_HALO, D), F32),
                        pltpu.VMEM((SUBLANES, tm + 2 * CONV_HALO - SUBLANES, DH), F32),
                        pltpu.VMEM((tm, D), F32)],
        compiler_params=_cp("parallel"),
        name="dwconv_ln",
    )(hx, hx, hx, w_dw, vec(b_dw), vec(ln_g), vec(ln_b))


def _rope_tables(b, t, tc):
    rows = t // GRID_W
    row = jnp.repeat(jnp.arange(rows), GRID_W).astype(F32)
    col = jnp.tile(jnp.arange(GRID_W), rows).astype(F32)
    pairs = ROPE // 4
    freqs = ROPE_BASE ** (-jnp.arange(pairs, dtype=F32) / pairs)
    ar = row[:, None] * freqs
    ac = col[:, None] * freqs
    cos = jnp.concatenate([jnp.cos(ar), jnp.cos(ar), jnp.cos(ac), jnp.cos(ac), jnp.ones((t, DH - ROPE), F32)], axis=1)
    sin = jnp.concatenate([-jnp.sin(ar), jnp.sin(ar), -jnp.sin(ac), jnp.sin(ac), jnp.zeros((t, DH - ROPE), F32)], axis=1)
    cos = jnp.concatenate([jnp.tile(cos, (b, 1)), jnp.ones((b * tc, DH), F32)], axis=0)
    sin = jnp.concatenate([jnp.tile(sin, (b, 1)), jnp.zeros((b * tc, DH), F32)], axis=0)
    return cos, sin


def _pick(n, *cands):
    for cand in cands:
        if n % cand == 0:
            return cand
    raise ValueError(f"no tile for {n}")


def kernel(x, c, ctx, c_ctx, w_mod, b_mod, norm_gains, w_in_ab, mla_q_norm, w_q_up, mla_kv_norm, w_kv_up, hgrn_lb, hgrn_norm, w_out_ab, conv_w_pw1, conv_b_pw1, conv_w_dw, conv_b_dw, conv_ln_g, conv_ln_b, conv_w_pw2, conv_b_pw2, ffn_w_up, ffn_w_conv, ffn_b_conv, ffn_w_down):
    b, t, _ = x.shape
    tc = ctx.shape[1]
    depth = w_mod.shape[0]
    n_lat = b * t
    n_ctx = b * tc
    assert x.shape[2] == D and depth % 2 == 0 and b < 8
    assert t % 512 == 0 and tc % 256 == 0 and n_ctx % 512 == 0 and t % n_ctx == 0
    assert t & (t - 1) == 0 and tc & (tc - 1) == 0
    tg = _pick(n_ctx, 1024, 512)
    tr = RES_TM
    lat_tiles = n_lat // tr
    ctx_tiles = n_ctx // tr
    w_out_bf, w_pw2_bf, w_down_bf = w_out_ab.astype(BF16), conv_w_pw2.astype(BF16), ffn_w_down.astype(BF16)
    w_in_p = repack_in_proj(w_in_ab)

    cc = jnp.concatenate([c, c_ctx[None, :], jnp.zeros((7 - b, D), F32)], axis=0)
    mods = modulation(cc, w_mod, b_mod).reshape(depth, 8, 1, 6 * D)
    cos, sin = _rope_tables(b, t, tc)
    masks_np, lvl_np = _gla_tables()
    masks = jnp.asarray(masks_np, BF16)
    lvl = jnp.asarray(lvl_np)

    u, hcur = norm_mod(x.reshape(n_lat, D), ctx.reshape(n_ctx, D), norm_gains[0, 0], mods[0], t, shift=0, scale=1)
    for l in range(depth):
        last = l == depth - 1
        j = l // 2
        m = n_lat if last else n_lat + n_ctx
        m_out = n_lat if l >= depth - 2 else n_lat + n_ctx
        row_tiles = m_out // tr
        g_pre1, g_post1, g_pre2, g_post2 = norm_gains[l]
        if l % 2 == 0:
            wq = jnp.pad(w_q_up[j].reshape(QL, H, DH + ROPE), ((0, 0), (0, 0), (0, DH - ROPE)))
            wq = wq.reshape(QL, 2 * H * DH).astype(BF16)
            wkv = w_kv_up[j].reshape(KVL, H, 2 * DH)
            wkv = jnp.concatenate([wkv[:, :, :DH].reshape(KVL, H * DH), wkv[:, :, DH:].reshape(KVL, H * DH)], axis=1)
            p = matmul(u, w_in_p, j, _pick(m, 1024, 512), 512)
            q, k, v = mla_prep(p, mla_q_norm[j], wq, mla_kv_norm[j], wkv.astype(BF16), cos, sin)
            att_lat = attention_latent(q, k, v, b, t, tc)
            o_f, o_b = gla(p, hgrn_lb, masks, lvl, b, t, tc, j)
            hg = hg_merge(o_f, o_b, p, hgrn_norm[j])
            att_pieces = [(att_lat, 0, lat_tiles)]
            if m_out > n_lat:
                att_pieces.append((attention_context(q, k, v, b, t, tc), lat_tiles, ctx_tiles))
            groups = [att_pieces, [(hg, 0, row_tiles)]]
            w_o, b_o = w_out_bf, None
        else:
            glu = matmul_gated(u, conv_w_pw1, j, None, conv_b_pw1, m=m, tm=tg, tn=512, n_lat=n_lat, seq_lat=t,
                               seq_ctx=tc, silu=False, out_dtype=F32, name="conformer_glu")
            mix = dwconv_ln(glu, conv_w_dw[j], conv_b_dw[j], conv_ln_g[j], conv_ln_b[j], n_lat, t)
            groups = [[(mix, 0, row_tiles)]]
            w_o, b_o = w_pw2_bf, conv_b_pw2[j]
        hcur, u = matmul_residual(groups, w_o, j, b_o, hcur, m_out, g_post1, mods[l], 2, n_lat, t, tr,
                                  nxt=(g_pre2, mods[l], 3, 4))
        act = matmul_gated(u, ffn_w_up, l, ffn_w_conv, ffn_b_conv, m=m_out, tm=tg, tn=512, n_lat=n_lat, seq_lat=t,
                           seq_ctx=tc, silu=True, out_dtype=BF16, name="ffn_up")
        nxt = None if last else (norm_gains[l + 1, 0], mods[l + 1], 0, 1)
        hcur, u = matmul_residual([[(act, 0, row_tiles)]], w_down_bf, l, None, hcur, m_out, g_post2, mods[l], 5,
                                  n_lat, t, tr, nxt=nxt)
    return hcur.reshape(b, t, D)
```
